```python
import jax, jax.numpy as jnp
from jax import lax
import numpy as np

D_MODEL = 1024
BATCH = 2
SEQ = 8192
DEPTH = 4
DEC_BATCH = 32
DEC_SEQ = 4
PAST_LEN = 8192
PAGE_SIZE = 128

N_MIXERS = 3
N_LAYERS_A = (DEPTH + 2) // 3
N_LAYERS_B = (DEPTH + 1) // 3
N_LAYERS_C = DEPTH // 3

N_HEADS_A = 16
HEAD_DIM_A = D_MODEL // N_HEADS_A
WIDTH_A = N_HEADS_A * HEAD_DIM_A
Q_BLOCK = 128
ATTN_SCALE = HEAD_DIM_A ** -0.5

EXPAND = 2
WIDTH_B = EXPAND * D_MODEL
POOL_WINDOWS = (2, 4, 8, 16)
N_GROUPS_B = len(POOL_WINDOWS)
GROUP_B = WIDTH_B // N_GROUPS_B
POOL_HIST = max(POOL_WINDOWS) - 1

WIDTH_C = EXPAND * D_MODEL
CHUNK = 128
N_GROUPS_C = 4
GROUP_C = WIDTH_C // N_GROUPS_C

DN_ALPHA = (2 * DEPTH) ** 0.25
DN_BETA = (8 * DEPTH) ** -0.25
LN_EPS = 1e-5

kernel_name = 'hybrid_fox_pool_gmlp_decode_step'


def layer_norm(x, g, b):
    xf = x.astype(jnp.float32)
    mu = jnp.mean(xf, axis=-1, keepdims=True)
    var = jnp.mean(jnp.square(xf - mu), axis=-1, keepdims=True)
    y = (xf - mu) * lax.rsqrt(var + LN_EPS) * g.astype(jnp.float32) + b.astype(jnp.float32)
    return y.astype(x.dtype)


def fox_project(x, w_in, b_f):
    h = x @ w_in
    q, k, v, gate, f = jnp.split(h, [WIDTH_A, 2 * WIDTH_A, 3 * WIDTH_A, 4 * WIDTH_A], axis=-1)
    logf = jax.nn.log_sigmoid((f + b_f).astype(jnp.float32))
    shp = x.shape[:-1] + (N_HEADS_A, HEAD_DIM_A)
    return q.reshape(shp), k.reshape(shp), v.reshape(shp), gate, logf


def fox_block(q, k, v, cq, ck, q_pos):
    s = jnp.einsum('bqhd,bkhd->bhqk', q, k, preferred_element_type=jnp.float32) * ATTN_SCALE
    s = s + (cq[..., :, None] - ck[..., None, :])
    mask = jnp.arange(k.shape[1])[None, :] <= q_pos[:, None]
    p = jax.nn.softmax(jnp.where(mask, s, -jnp.inf), axis=-1)
    return jnp.einsum('bhqk,bkhd->bqhd', p.astype(v.dtype), v)


def fox_prompt(x, w_in, b_f, w_out):
    bsz, seq = x.shape[:2]
    q, k, v, gate, logf = fox_project(x, w_in, b_f)
    c = jnp.cumsum(logf, axis=1).transpose(0, 2, 1)
    nb = seq // Q_BLOCK
    qb = q.reshape(bsz, nb, Q_BLOCK, N_HEADS_A, HEAD_DIM_A).swapaxes(0, 1)
    cb = c.reshape(bsz, N_HEADS_A, nb, Q_BLOCK).transpose(2, 0, 1, 3)
    pos = jnp.arange(seq).reshape(nb, Q_BLOCK)
    o = lax.map(lambda blk: fox_block(blk[0], k, v, blk[1], c, blk[2]), (qb, cb, pos))
    o = o.swapaxes(0, 1).reshape(bsz, seq, WIDTH_A)
    y = (o * jax.nn.silu(gate)) @ w_out
    return y, k, v, logf.astype(x.dtype)


def fox_sample(x, k_cache, v_cache, logf_cache, page_table, w_in, b_f, w_out):
    bsz, t = x.shape[:2]
    past = page_table.shape[1] * k_cache.shape[1]
    q, k, v, gate, logf = fox_project(x, w_in, b_f)
    k_past = k_cache[page_table].reshape(bsz, past, N_HEADS_A, HEAD_DIM_A)
    v_past = v_cache[page_table].reshape(bsz, past, N_HEADS_A, HEAD_DIM_A)
    f_past = logf_cache[page_table].reshape(bsz, past, N_HEADS_A).astype(jnp.float32)
    k_all = jnp.concatenate([k_past, k.astype(k_past.dtype)], axis=1)
    v_all = jnp.concatenate([v_past, v.astype(v_past.dtype)], axis=1)
    c = jnp.cumsum(jnp.concatenate([f_past, logf], axis=1), axis=1).transpose(0, 2, 1)
    o = fox_block(q, k_all, v_all, c[:, :, past:], c, past + jnp.arange(t))
    y = (o.reshape(bsz, t, WIDTH_A) * jax.nn.silu(gate)) @ w_out
    return y, k, v, logf.astype(x.dtype)


def pool_minus_self(a_ext, start_pos):
    t = a_ext.shape[1] - POOL_HIST
    cs = jnp.cumsum(a_ext.astype(jnp.float32), axis=1)
    cs = jnp.pad(cs, ((0, 0), (1, 0), (0, 0)))
    end = cs[:, POOL_HIST + 1:]
    pos = start_pos + jnp.arange(t)
    means = []
    for g, w in enumerate(POOL_WINDOWS):
        sl = slice(g * GROUP_B, (g + 1) * GROUP_B)
        win = end[..., sl] - cs[:, POOL_HIST + 1 - w: POOL_HIST + 1 - w + t, sl]
        cnt = jnp.minimum(pos + 1, w).astype(jnp.float32)[None, :, None]
        means.append(win / cnt)
    m = jnp.concatenate(means, axis=-1)
    return (m - a_ext[:, POOL_HIST:].astype(jnp.float32)).astype(a_ext.dtype)


def pool_mixer(x, hist, start_pos, w_in, w_grp, b_grp, scale, w_out):
    bsz, t = x.shape[:2]
    a, gate = jnp.split(x @ w_in, 2, axis=-1)
    a_ext = jnp.concatenate([hist.astype(a.dtype), a], axis=1)
    m = pool_minus_self(a_ext, start_pos).reshape(bsz, t, N_GROUPS_B, GROUP_B)
    z = jnp.einsum('btgc,gcd->btgd', m, w_grp).reshape(bsz, t, WIDTH_B) + b_grp
    y = (z * scale * jax.nn.silu(gate)) @ w_out
    return y, a_ext[:, -POOL_HIST:]


def gmlp_mixer(x, w_in, ln_g, ln_b, w_s, b_s, w_out):
    bsz, t = x.shape[:2]
    u, v, gate = jnp.split(x @ w_in, 3, axis=-1)
    v = layer_norm(v, ln_g, ln_b)
    L = min(t, CHUNK)
    vc = v.reshape(bsz, t // L, L, N_GROUPS_C, GROUP_C)
    w = jnp.tril(w_s[:, :L, :L])
    sv = jnp.einsum('gij,bnjgc->bnigc', w, vc) + b_s[:, :L].T[None, None, :, :, None]
    y = (u * sv.reshape(bsz, t, WIDTH_C) * jax.nn.silu(gate)) @ w_out
    return y, v


def setup_inputs(seed: int = 0) -> dict:
    key = jax.random.key(seed)
    ks = iter(jax.random.split(key, 40))

    def nrm(shape, s=1.0):
        return jax.random.normal(next(ks), shape, jnp.float32) * s

    n_pages = PAST_LEN // PAGE_SIZE
    n_pool = (DEC_BATCH * n_pages * 5) // 4
    perm = jax.random.permutation(next(ks), n_pool)
    page_table = perm[:DEC_BATCH * n_pages].reshape(DEC_BATCH, n_pages).astype(jnp.int32)

    x_prompt = nrm((BATCH, SEQ, D_MODEL))
    x_sample = nrm((DEC_BATCH, DEC_SEQ, D_MODEL))

    b_f_a = jax.random.uniform(next(ks), (N_LAYERS_A, N_HEADS_A), jnp.float32, 1.0, 6.0)
    cache_k = nrm((N_LAYERS_A, n_pool, PAGE_SIZE, N_HEADS_A, HEAD_DIM_A))
    cache_v = nrm((N_LAYERS_A, n_pool, PAGE_SIZE, N_HEADS_A, HEAD_DIM_A))
    cache_logf = jax.nn.log_sigmoid(b_f_a[:, None, None, :] + nrm((N_LAYERS_A, n_pool, PAGE_SIZE, N_HEADS_A)))
    state_pool = nrm((N_LAYERS_B, DEC_BATCH, POOL_HIST, WIDTH_B))

    ln_g = 1.0 + nrm((DEPTH, D_MODEL), 0.1)
    ln_b = nrm((DEPTH, D_MODEL), 0.01)

    col_scale_a = jnp.concatenate([jnp.ones((2 * WIDTH_A,), jnp.float32),
                                   jnp.full((WIDTH_A,), DN_BETA, jnp.float32),
                                   jnp.ones((WIDTH_A + N_HEADS_A,), jnp.float32)])
    w_in_a = nrm((N_LAYERS_A, D_MODEL, 4 * WIDTH_A + N_HEADS_A), D_MODEL ** -0.5) * col_scale_a
    w_out_a = nrm((N_LAYERS_A, WIDTH_A, D_MODEL), WIDTH_A ** -0.5 * DN_BETA)

    w_in_b = nrm((N_LAYERS_B, D_MODEL, 2 * WIDTH_B), D_MODEL ** -0.5)
    w_grp_b = nrm((N_LAYERS_B, N_GROUPS_B, GROUP_B, GROUP_B), GROUP_B ** -0.5)
    b_grp_b = nrm((N_LAYERS_B, WIDTH_B), 0.01)
    scale_b = 1.0 + nrm((N_LAYERS_B, WIDTH_B), 0.1)
    w_out_b = nrm((N_LAYERS_B, WIDTH_B, D_MODEL), WIDTH_B ** -0.5 * DN_BETA)

    w_in_c = nrm((N_LAYERS_C, D_MODEL, 3 * WIDTH_C), D_MODEL ** -0.5)
    ln_v_g = 1.0 + nrm((N_LAYERS_C, WIDTH_C), 0.1)
    ln_v_b = nrm((N_LAYERS_C, WIDTH_C), 0.01)
    w_s_c = nrm((N_LAYERS_C, N_GROUPS_C, CHUNK, CHUNK), CHUNK ** -0.5)
    b_s_c = 1.0 + nrm((N_LAYERS_C, N_GROUPS_C, CHUNK), 0.1)
    w_out_c = nrm((N_LAYERS_C, WIDTH_C, D_MODEL), WIDTH_C ** -0.5 * DN_BETA)

    return {'x_prompt': x_prompt, 'x_sample': x_sample,
            'cache_k': cache_k, 'cache_v': cache_v, 'cache_logf': cache_logf,
            'state_pool': state_pool, 'page_table': page_table,
            'ln_g': ln_g, 'ln_b': ln_b,
            'w_in_a': w_in_a, 'b_f_a': b_f_a, 'w_out_a': w_out_a,
            'w_in_b': w_in_b, 'w_grp_b': w_grp_b, 'b_grp_b': b_grp_b, 'scale_b': scale_b, 'w_out_b': w_out_b,
            'w_in_c': w_in_c, 'ln_v_g': ln_v_g, 'ln_v_b': ln_v_b, 'w_s_c': w_s_c, 'b_s_c': b_s_c,
            'w_out_c': w_out_c}


def reference(x_prompt, x_sample, cache_k, cache_v, cache_logf, state_pool, page_table,
              ln_g, ln_b, w_in_a, b_f_a, w_out_a,
              w_in_b, w_grp_b, b_grp_b, scale_b, w_out_b,
              w_in_c, ln_v_g, ln_v_b, w_s_c, b_s_c, w_out_c):
    past_len = page_table.shape[1] * cache_k.shape[2]
    xp, xs = x_prompt, x_sample
    kp_l, vp_l, fp_l, ks_l, vs_l, fs_l = [], [], [], [], [], []
    poolp_l, pools_l, chunkv_l = [], [], []
    for i in range(DEPTH):
        j = i // N_MIXERS
        kind = i % N_MIXERS
        if kind == 0:
            yp, kp, vp, fp = fox_prompt(xp, w_in_a[j], b_f_a[j], w_out_a[j])
            ys, ks, vs, fs = fox_sample(xs, cache_k[j], cache_v[j], cache_logf[j], page_table,
                                        w_in_a[j], b_f_a[j], w_out_a[j])
            kp_l.append(kp); vp_l.append(vp); fp_l.append(fp)
            ks_l.append(ks); vs_l.append(vs); fs_l.append(fs)
        elif kind == 1:
            hist0 = jnp.zeros((xp.shape[0], POOL_HIST, WIDTH_B), xp.dtype)
            yp, hp = pool_mixer(xp, hist0, 0, w_in_b[j], w_grp_b[j], b_grp_b[j], scale_b[j], w_out_b[j])
            ys, hs = pool_mixer(xs, state_pool[j], past_len, w_in_b[j], w_grp_b[j], b_grp_b[j],
                                scale_b[j], w_out_b[j])
            poolp_l.append(hp); pools_l.append(hs)
        else:
            yp, _ = gmlp_mixer(xp, w_in_c[j], ln_v_g[j], ln_v_b[j], w_s_c[j], b_s_c[j], w_out_c[j])
            ys, vrows = gmlp_mixer(xs, w_in_c[j], ln_v_g[j], ln_v_b[j], w_s_c[j], b_s_c[j], w_out_c[j])
            chunkv_l.append(vrows)
        xp = layer_norm(DN_ALPHA * xp + yp, ln_g[i], ln_b[i])
        xs = layer_norm(DN_ALPHA * xs + ys, ln_g[i], ln_b[i])
    return (xp, xs,
            jnp.stack(kp_l), jnp.stack(vp_l), jnp.stack(fp_l),
            jnp.stack(ks_l), jnp.stack(vs_l), jnp.stack(fs_l),
            jnp.stack(poolp_l), jnp.stack(pools_l),
            jnp.stack(chunkv_l))
```

```python
import functools

import jax
import jax.numpy as jnp
from jax import lax
from jax.experimental import pallas as pl
from jax.experimental.pallas import tpu as pltpu

D_MODEL = 1024
DEPTH = 4
N_MIXERS = 3
N_HEADS = 16
HEAD_DIM = 64
ATTN_SCALE = HEAD_DIM ** -0.5
WIDTH_B = 2048
POOL_WINDOWS = (2, 4, 8, 16)
GROUP_B = 512
POOL_HIST = 15
WIDTH_C = 2048
CHUNK = 128
N_GROUPS_C = 4
GROUP_C = 512
DN_ALPHA = (2 * DEPTH) ** 0.25
LN_EPS = 1e-5

LANES = 128
HIST_PAD = 16
MASK_VALUE = -1e30
VMEM_LIMIT_BYTES = 56 * 1024 * 1024

F32 = jnp.float32
BF16 = jnp.bfloat16


def _params(*sem):
    return pltpu.CompilerParams(dimension_semantics=sem, vmem_limit_bytes=VMEM_LIMIT_BYTES)


def _dot(a, b):
    return jnp.dot(a, b, preferred_element_type=F32)


def _silu(x):
    return x / (1.0 + jnp.exp(-x))


def _layer_norm(z, g, b):
    mu = jnp.mean(z, axis=-1, keepdims=True)
    d = z - mu
    var = jnp.mean(d * d, axis=-1, keepdims=True)
    return d * lax.rsqrt(var + LN_EPS) * g + b


def _split3(x):
    hi = x.astype(BF16)
    r1 = x - hi.astype(F32)
    mid = r1.astype(BF16)
    lo = (r1 - mid.astype(F32)).astype(BF16)
    return hi, mid, lo


def _const_spec(shape):
    return pl.BlockSpec(shape, lambda *_: (0,) * len(shape))


def _fox_proj_kernel(x_ref, w_ref, wf_ref, bf_ref, tri_ref,
                     q_ref, k32_ref, kbf_ref, v32_ref, g_ref, logf_ref, caug_ref, carry_ref):
    @pl.when(pl.program_id(1) == 0)
    def _():
        carry_ref[...] = jnp.zeros_like(carry_ref)

    xb = x_ref[...].astype(BF16)
    q_ref[...] = (_dot(xb, w_ref[:, 0:1024]) * ATTN_SCALE).astype(BF16)
    k = _dot(xb, w_ref[:, 1024:2048])
    k32_ref[...] = k
    kbf_ref[...] = k.astype(BF16)
    v32_ref[...] = _dot(xb, w_ref[:, 2048:3072])
    g_ref[...] = _dot(xb, w_ref[:, 3072:4096])

    f = _dot(xb, wf_ref[...]) + bf_ref[...]
    lf = jnp.minimum(f, 0.0) - jnp.log1p(jnp.exp(-jnp.abs(f)))
    logf_ref[...] = lf[:, :N_HEADS]
    hi, mid, lo = _split3(lf)
    tri = tri_ref[...]
    c = _dot(tri, hi) + _dot(tri, mid) + _dot(tri, lo) + carry_ref[...]
    carry_ref[...] = c[c.shape[0] - 1:, :]
    nhi = (-c).astype(BF16).astype(F32)
    rem = -c - nhi
    nmid = rem.astype(BF16).astype(F32)
    lane = lax.broadcasted_iota(jnp.int32, c.shape, 1)
    terms = jnp.where(lane < N_HEADS, nhi,
                      jnp.where(lane < 2 * N_HEADS, nmid,
                                jnp.where(lane < 3 * N_HEADS, rem - nmid, 0.0)))
    caug_ref[...] = terms.astype(BF16)


def _fox_proj(x2d, w_main, w_f, b_f, n_seq, tm):
    m = x2d.shape[0]
    nt = m // (n_seq * tm)
    tri = jnp.tril(jnp.ones((tm, tm), BF16))
    row = lambda b, i: (b * nt + i, 0)
    wide = pl.BlockSpec((tm, 1024), row)
    narrow = pl.BlockSpec((tm, N_HEADS), row)
    return pl.pallas_call(
        _fox_proj_kernel,
        grid=(n_seq, nt),
        in_specs=[wide, _const_spec((1024, 4096)), _const_spec((1024, LANES)),
                  _const_spec((1, LANES)), _const_spec((tm, tm))],
        out_specs=[wide, wide, wide, wide, wide, narrow, pl.BlockSpec((tm, LANES), row)],
        out_shape=[jax.ShapeDtypeStruct((m, 1024), BF16),
                   jax.ShapeDtypeStruct((m, 1024), F32),
                   jax.ShapeDtypeStruct((m, 1024), BF16),
                   jax.ShapeDtypeStruct((m, 1024), F32),
                   jax.ShapeDtypeStruct((m, 1024), F32),
                   jax.ShapeDtypeStruct((m, N_HEADS), F32),
                   jax.ShapeDtypeStruct((m, LANES), BF16)],
        scratch_shapes=[pltpu.VMEM((1, LANES), F32)],
        compiler_params=_params("arbitrary", "arbitrary"),
        name="fox_proj",
    )(x2d, w_main, w_f, b_f, tri)


def _fox_attn_kernel(qt_ref, k_ref, caug_ref, vt_ref, o_ref, w_ref, m_ref, l_ref, acc_ref, *, tq, tk):
    pair = pl.program_id(1)
    qi = pl.program_id(2)
    qt2 = qt_ref[0]
    row = lax.broadcasted_iota(jnp.int32, (LANES, tq), 0)
    z64 = jnp.zeros((HEAD_DIM, tq), BF16)
    for h in range(2):
        head = 2 * pair + h
        sel = (row == head) | (row == head + N_HEADS) | (row == head + 2 * N_HEADS)
        sel = jnp.where(sel, 1.0, 0.0).astype(BF16)
        top = [qt2[0:HEAD_DIM], z64] if h == 0 else [z64, qt2[HEAD_DIM:]]
        w_ref[h] = jnp.concatenate(top + [sel], axis=0)
    m_ref[...] = jnp.full_like(m_ref, MASK_VALUE)
    l_ref[...] = jnp.zeros_like(l_ref)
    acc_ref[...] = jnp.zeros_like(acc_ref)

    def tile(kt, masked):
        ks = pl.multiple_of(kt * tk, tk)
        kaug = jnp.concatenate([k_ref[0, pl.ds(ks, tk), :], caug_ref[0, pl.ds(ks, tk), :]], axis=1)
        vt = vt_ref[0, kt]
        for h in range(2):
            s = _dot(kaug, w_ref[h])
            if masked:
                kidx = ks + lax.broadcasted_iota(jnp.int32, (tk, tq), 0)
                qidx = qi * tq + lax.broadcasted_iota(jnp.int32, (tk, tq), 1)
                s = jnp.where(kidx <= qidx, s, MASK_VALUE)
            m_prev = m_ref[h]
            m_new = jnp.maximum(m_prev, jnp.max(s, axis=0, keepdims=True))
            alpha = jnp.exp(m_prev - m_new)
            p = jnp.exp(s - m_new)
            l_ref[h] = alpha * l_ref[h] + jnp.sum(p, axis=0, keepdims=True)
            pv = _dot(vt[h * HEAD_DIM:(h + 1) * HEAD_DIM], p.astype(BF16))
            acc_ref[h] = alpha * acc_ref[h] + pv
            m_ref[h] = m_new

    n_full = qi * (tq // tk)

    def body(kt, carry):
        tile(kt, False)
        return carry

    lax.fori_loop(0, n_full, body, 0)
    for d in range(tq // tk):
        tile(n_full + d, True)
    o2 = jnp.concatenate([acc_ref[0] / l_ref[0], acc_ref[1] / l_ref[1]], axis=0)
    o_ref[0] = o2.T


def _fox_attn(qt, kbf, caug, vt, tq, tk):
    n_seq, _, seq = qt.shape
    nk = seq // tk
    return pl.pallas_call(
        functools.partial(_fox_attn_kernel, tq=tq, tk=tk),
        grid=(n_seq, N_HEADS // 2, seq // tq),
        in_specs=[pl.BlockSpec((1, LANES, tq), lambda b, p, i: (b, p, i)),
                  pl.BlockSpec((1, seq, LANES), lambda b, p, i: (b, 0, p)),
                  pl.BlockSpec((1, seq, LANES), lambda b, p, i: (b, 0, 0)),
                  pl.BlockSpec((1, nk, LANES, tk), lambda b, p, i: (b, 0, p, 0))],
        out_specs=pl.BlockSpec((1, tq, LANES), lambda b, p, i: (b, i, p)),
        out_shape=jax.ShapeDtypeStruct((n_seq, seq, 1024), F32),
        scratch_shapes=[pltpu.VMEM((2, 2 * LANES, tq), BF16),
                        pltpu.VMEM((2, 1, tq), F32),
                        pltpu.VMEM((2, 1, tq), F32),
                        pltpu.VMEM((2, HEAD_DIM, tq), F32)],
        compiler_params=_params("arbitrary", "arbitrary", "arbitrary"),
        name="fox_attn",
    )(qt, kbf, caug, vt)


def _fox_decode_kernel(pt_ref, q_ref, kn_ref, vn_ref, lfn_ref, kp_ref, vp_ref, lfp_ref, tri_ref,
                       o_ref, m_ref, l_ref, acc_ref, r_ref, *, n_pages):
    del pt_ref
    step = pl.program_id(1)

    def attend(load_k, load_v, bias, mask):
        for h in range(N_HEADS):
            s = lax.dot_general(q_ref[0, h], load_k(h).astype(BF16), (((1,), (1,)), ((), ())),
                                preferred_element_type=F32)
            s = s + bias[h:h + 1, :]
            if mask is not None:
                s = jnp.where(mask, s, MASK_VALUE)
            m_prev = m_ref[h]
            m_new = jnp.maximum(m_prev, jnp.max(s, axis=-1, keepdims=True))
            alpha = jnp.exp(m_prev - m_new)
            p = jnp.exp(s - m_new)
            l_ref[h] = alpha * l_ref[h] + jnp.sum(p, axis=-1, keepdims=True)
            pv = _dot(p.astype(BF16), load_v(h).astype(BF16))
            acc_ref[h] = alpha[:, :HEAD_DIM] * acc_ref[h] + pv
            m_ref[h] = m_new

    def lane_sums(lf, tri):
        hi, mid, lo = _split3(lf)
        return _dot(hi, tri) + _dot(mid, tri) + _dot(lo, tri)

    col = lax.broadcasted_iota(jnp.int32, (LANES, LANES), 1)
    rowi = lax.broadcasted_iota(jnp.int32, (LANES, LANES), 0)

    @pl.when(step == 0)
    def _():
        m_ref[...] = jnp.full_like(m_ref, MASK_VALUE)
        l_ref[...] = jnp.zeros_like(l_ref)
        acc_ref[...] = jnp.zeros_like(acc_ref)
        r_ref[...] = jnp.zeros_like(r_ref)
        incl = jnp.where(rowi <= col, 1.0, 0.0).astype(BF16)
        bias = -lane_sums(lfn_ref[0], incl)
        t_idx = lax.broadcasted_iota(jnp.int32, (8, LANES), 0)
        j_idx = lax.broadcasted_iota(jnp.int32, (8, LANES), 1)
        attend(lambda h: kn_ref[0, h], lambda h: vn_ref[0, h], bias, j_idx <= t_idx)

    @pl.when(step > 0)
    def _():
        lf = lfp_ref[0, 0]
        bias = lane_sums(lf, tri_ref[...]) + r_ref[...]
        r_ref[...] = r_ref[...] + jnp.sum(lf, axis=-1, keepdims=True)
        attend(lambda h: kp_ref[0, 0, pl.ds(h, LANES, stride=N_HEADS), :],
               lambda h: vp_ref[0, 0, pl.ds(h, LANES, stride=N_HEADS), :], bias, None)

    @pl.when(step == n_pages)
    def _():
        for h in range(N_HEADS):
            o_ref[0, h] = acc_ref[h] / l_ref[h][:, :HEAD_DIM]


def _fox_decode(page_table, q, kn, vn, lfn, cache_k, cache_v, cache_lft, layer):
    n_b, n_pages = page_table.shape
    strict = jnp.where(lax.broadcasted_iota(jnp.int32, (LANES, LANES), 0)
                       > lax.broadcasted_iota(jnp.int32, (LANES, LANES), 1), 1.0, 0.0).astype(BF16)

    def page(b, s, pt):
        return pt[b, n_pages - jnp.maximum(s, 1)]

    per_b = lambda shape: pl.BlockSpec((1,) + shape, lambda b, s, pt: (b,) + (0,) * len(shape))
    rows = LANES * N_HEADS
    grid_spec = pltpu.PrefetchScalarGridSpec(
        num_scalar_prefetch=1,
        grid=(n_b, n_pages + 1),
        in_specs=[per_b((N_HEADS, 8, HEAD_DIM)),
                  per_b((N_HEADS, LANES, HEAD_DIM)),
                  per_b((N_HEADS, LANES, HEAD_DIM)),
                  per_b((N_HEADS, LANES)),
                  pl.BlockSpec((1, 1, rows, HEAD_DIM), lambda b, s, pt: (layer, page(b, s, pt), 0, 0)),
                  pl.BlockSpec((1, 1, rows, HEAD_DIM), lambda b, s, pt: (layer, page(b, s, pt), 0, 0)),
                  pl.BlockSpec((1, 1, N_HEADS, LANES), lambda b, s, pt: (layer, page(b, s, pt), 0, 0)),
                  pl.BlockSpec((LANES, LANES), lambda b, s, pt: (0, 0))],
        out_specs=per_b((N_HEADS, 8, HEAD_DIM)),
        scratch_shapes=[pltpu.VMEM((N_HEADS, 8, LANES), F32),
                        pltpu.VMEM((N_HEADS, 8, LANES), F32),
                        pltpu.VMEM((N_HEADS, 8, HEAD_DIM), F32),
                        pltpu.VMEM((N_HEADS, LANES), F32)])
    return pl.pallas_call(
        functools.partial(_fox_decode_kernel, n_pages=n_pages),
        grid_spec=grid_spec,
        out_shape=jax.ShapeDtypeStruct((n_b, N_HEADS, 8, HEAD_DIM), F32),
        compiler_params=_params("arbitrary", "arbitrary"),
        name="fox_decode",
    )(page_table, q, kn, vn, lfn, cache_k, cache_v, cache_lft, strict)


def _fox_out_kernel(o_ref, g_ref, x_ref, w_ref, lng_ref, lnb_ref, y_ref):
    a = (o_ref[...] * _silu(g_ref[...])).astype(BF16)
    z = DN_ALPHA * x_ref[...] + _dot(a, w_ref[...])
    y_ref[...] = _layer_norm(z, lng_ref[...], lnb_ref[...])


def _fox_out(o2d, g2d, x2d, w_out, ln_g, ln_b, tm):
    m = x2d.shape[0]
    wide = pl.BlockSpec((tm, 1024), lambda i: (i, 0))
    return pl.pallas_call(
        _fox_out_kernel,
        grid=(m // tm,),
        in_specs=[wide, wide, wide, _const_spec((1024, 1024)), _const_spec((1, 1024)),
                  _const_spec((1, 1024))],
        out_specs=wide,
        out_shape=jax.ShapeDtypeStruct((m, 1024), F32),
        compiler_params=_params("arbitrary"),
        name="fox_out",
    )(o2d, g2d, x2d, w_out, ln_g, ln_b)


def _pool_tail(m_bf, gate, x, wg_ref, bg_ref, sc_ref, wo_ref, lng_ref, lnb_ref):
    z = jnp.concatenate([_dot(m_bf[:, g * GROUP_B:(g + 1) * GROUP_B], wg_ref[g])
                         for g in range(len(POOL_WINDOWS))], axis=1) + bg_ref[...]
    a = (z * sc_ref[...] * _silu(gate)).astype(BF16)
    zz = DN_ALPHA * x + _dot(a, wo_ref[...])
    return _layer_norm(zz, lng_ref[...], lnb_ref[...])


def _pool_prompt_kernel(x_ref, wi_ref, wg_ref, bg_ref, sc_ref, wo_ref, lng_ref, lnb_ref,
                        y_ref, hist_ref, ext_ref, m_ref, *, tm):
    i = pl.program_id(1)

    @pl.when(i == 0)
    def _():
        ext_ref[0:HIST_PAD, :] = jnp.zeros((HIST_PAD, WIDTH_B), F32)

    x = x_ref[...]
    xb = x.astype(BF16)
    a = _dot(xb, wi_ref[:, 0:WIDTH_B])
    ext_ref[HIST_PAD:, :] = a
    for g, w in enumerate(POOL_WINDOWS):
        cols = slice(g * GROUP_B, (g + 1) * GROUP_B)
        win = a[:, cols]
        for d in range(1, w):
            win = win + ext_ref[HIST_PAD - d:HIST_PAD - d + tm, cols]
        m_ref[:, cols] = (win * (1.0 / w) - a[:, cols]).astype(BF16)

    @pl.when(i == 0)
    def _():
        pos = lax.broadcasted_iota(jnp.int32, (HIST_PAD, GROUP_B), 0)
        for g, w in enumerate(POOL_WINDOWS):
            cols = slice(g * GROUP_B, (g + 1) * GROUP_B)
            win = ext_ref[HIST_PAD:2 * HIST_PAD, cols]
            for d in range(1, w):
                win = win + ext_ref[HIST_PAD - d:2 * HIST_PAD - d, cols]
            cnt = jnp.minimum(pos + 1, w).astype(F32)
            m_ref[0:HIST_PAD, cols] = (win / cnt - ext_ref[HIST_PAD:2 * HIST_PAD, cols]).astype(BF16)

    gate = _dot(xb, wi_ref[:, WIDTH_B:])
    y_ref[...] = _pool_tail(m_ref[...], gate, x, wg_ref, bg_ref, sc_ref, wo_ref, lng_ref, lnb_ref)
    hist_ref[0] = a[tm - HIST_PAD:, :]
    ext_ref[0:HIST_PAD, :] = a[tm - HIST_PAD:, :]


def _pool_prompt(x2d, w_in, w_grp, b_grp, scale, w_out, ln_g, ln_b, n_seq, tm):
    m = x2d.shape[0]
    nt = m // (n_seq * tm)
    wide = pl.BlockSpec((tm, 1024), lambda b, i: (b * nt + i, 0))
    return pl.pallas_call(
        functools.partial(_pool_prompt_kernel, tm=tm),
        grid=(n_seq, nt),
        in_specs=[wide, _const_spec((1024, 2 * WIDTH_B)), _const_spec((4, GROUP_B, GROUP_B)),
                  _const_spec((1, WIDTH_B)), _const_spec((1, WIDTH_B)), _const_spec((WIDTH_B, 1024)),
                  _const_spec((1, 1024)), _const_spec((1, 1024))],
        out_specs=[wide, pl.BlockSpec((1, HIST_PAD, WIDTH_B), lambda b, i: (b, 0, 0))],
        out_shape=[jax.ShapeDtypeStruct((m, 1024), F32),
                   jax.ShapeDtypeStruct((n_seq, HIST_PAD, WIDTH_B), F32)],
        scratch_shapes=[pltpu.VMEM((HIST_PAD + tm, WIDTH_B), F32),
                        pltpu.VMEM((tm, WIDTH_B), BF16)],
        compiler_params=_params("arbitrary", "arbitrary"),
        name="pool_prompt",
    )(x2d, w_in, w_grp, b_grp, scale, w_out, ln_g, ln_b)


def _pool_sample_kernel(x_ref, hist_ref, wi_ref, wg_ref, bg_ref, sc_ref, wo_ref, lng_ref, lnb_ref,
                        y_ref, newhist_ref, *, n_new):
    xs = [x_ref[t] for t in range(n_new)]
    a = [_dot(x.astype(BF16), wi_ref[:, 0:WIDTH_B]) for x in xs]
    ext = [hist_ref[r] for r in range(POOL_HIST)] + a
    for r in range(POOL_HIST):
        newhist_ref[r] = ext[n_new + r]
    for t in range(n_new):
        parts = []
        for g, w in enumerate(POOL_WINDOWS):
            cols = slice(g * GROUP_B, (g + 1) * GROUP_B)
            win = a[t][:, cols]
            for d in range(1, w):
                win = win + ext[POOL_HIST + t - d][:, cols]
            parts.append(win * (1.0 / w) - a[t][:, cols])
        m_bf = jnp.concatenate(parts, axis=1).astype(BF16)
        gate = _dot(xs[t].astype(BF16), wi_ref[:, WIDTH_B:])
        y_ref[t] = _pool_tail(m_bf, gate, xs[t], wg_ref, bg_ref, sc_ref, wo_ref, lng_ref, lnb_ref)


def _pool_sample(x_tm, hist_tm, w_in, w_grp, b_grp, scale, w_out, ln_g, ln_b):
    n_new, n_b, _ = x_tm.shape
    full = lambda a: _const_spec(a.shape)
    args = (x_tm, hist_tm, w_in, w_grp, b_grp, scale, w_out, ln_g, ln_b)
    return pl.pallas_call(
        functools.partial(_pool_sample_kernel, n_new=n_new),
        grid=(1,),
        in_specs=[full(a) for a in args],
        out_specs=[_const_spec((n_new, n_b, 1024)), _const_spec((POOL_HIST, n_b, WIDTH_B))],
        out_shape=[jax.ShapeDtypeStruct((n_new, n_b, 1024), F32),
                   jax.ShapeDtypeStruct((POOL_HIST, n_b, WIDTH_B), F32)],
        compiler_params=_params("arbitrary"),
        name="pool_sample",
    )(*args)


def _gmlp_kernel(x_ref, wi_ref, vg_ref, vb_ref, ws_ref, bs_ref, wo_ref, lng_ref, lnb_ref,
                 *out_refs, tm, span, emit_v):
    y_ref = out_refs[0]
    x = x_ref[...]
    xb = x.astype(BF16)
    v = _layer_norm(_dot(xb, wi_ref[:, WIDTH_C:2 * WIDTH_C]), vg_ref[...], vb_ref[...])
    if emit_v:
        out_refs[1][...] = v
    vb16 = v.astype(BF16)
    r = lax.broadcasted_iota(jnp.int32, (CHUNK, CHUNK), 0)
    c = lax.broadcasted_iota(jnp.int32, (CHUNK, CHUNK), 1)
    keep = (c <= r) & ((r ^ c) < span)
    chunks = []
    for n in range(tm // CHUNK):
        rows = slice(n * CHUNK, (n + 1) * CHUNK)
        groups = []
        for g in range(N_GROUPS_C):
            wmix = jnp.where(keep, ws_ref[g], 0.0).astype(BF16)
            groups.append(_dot(wmix, vb16[rows, g * GROUP_C:(g + 1) * GROUP_C]))
        chunks.append(jnp.concatenate(groups, axis=1) + bs_ref[...])
    sv = jnp.concatenate(chunks, axis=0) if len(chunks) > 1 else chunks[0]
    u = _dot(xb, wi_ref[:, 0:WIDTH_C])
    gate = _dot(xb, wi_ref[:, 2 * WIDTH_C:])
    a = (u * sv * _silu(gate)).astype(BF16)
    z = DN_ALPHA * x + _dot(a, wo_ref[...])
    y_ref[...] = _layer_norm(z, lng_ref[...], lnb_ref[...])


def _gmlp(x2d, w_in, vg, vb, ws_tiled, bs_rows, w_out, ln_g, ln_b, tm, span, emit_v):
    m = x2d.shape[0]
    wide = pl.BlockSpec((tm, 1024), lambda i: (i, 0))
    out_specs = [wide]
    out_shape = [jax.ShapeDtypeStruct((m, 1024), F32)]
    if emit_v:
        out_specs.append(pl.BlockSpec((tm, WIDTH_C), lambda i: (i, 0)))
        out_shape.append(jax.ShapeDtypeStruct((m, WIDTH_C), F32))
    return pl.pallas_call(
        functools.partial(_gmlp_kernel, tm=tm, span=span, emit_v=emit_v),
        grid=(m // tm,),
        in_specs=[wide, _const_spec((1024, 3 * WIDTH_C)), _const_spec((1, WIDTH_C)),
                  _const_spec((1, WIDTH_C)), _const_spec((N_GROUPS_C, CHUNK, CHUNK)),
                  _const_spec((CHUNK, WIDTH_C)), _const_spec((WIDTH_C, 1024)),
                  _const_spec((1, 1024)), _const_spec((1, 1024))],
        out_specs=out_specs,
        out_shape=out_shape,
        compiler_params=_params("arbitrary"),
        name="gmlp_v" if emit_v else "gmlp",
    )(x2d, w_in, vg, vb, ws_tiled, bs_rows, w_out, ln_g, ln_b)


def _row(v):
    return v.reshape(1, -1)


def _fox_layer(xp, xs, cache_k, cache_v, cache_lft, page_table, layer, w_in, b_f, w_out, ln_g, ln_b):
    n_seq, seq, _ = xp.shape
    n_b, n_new, _ = xs.shape
    w_main = w_in[:, :4 * 1024].astype(BF16)
    w_f = jnp.pad(jnp.tile(w_in[:, 4 * 1024:], (1, 3)), ((0, 0), (0, LANES - 3 * N_HEADS))).astype(BF16)
    b_fp = jnp.pad(jnp.tile(b_f, 3), (0, LANES - 3 * N_HEADS)).reshape(1, LANES)
    w_o = w_out.astype(BF16)
    tq, tk = 512, 256

    x2d = xp.reshape(n_seq * seq, 1024)
    q, k32, kbf, v32, gate, logf, caug = _fox_proj(x2d, w_main, w_f, b_fp, n_seq, 512)
    qt = q.reshape(n_seq, seq, 1024).transpose(0, 2, 1)
    vt = v32.astype(BF16).reshape(n_seq, seq // tk, tk, 1024).transpose(0, 1, 3, 2)
    o = _fox_attn(qt, kbf.reshape(n_seq, seq, 1024), caug.reshape(n_seq, seq, LANES), vt, tq, tk)
    yp = _fox_out(o.reshape(n_seq * seq, 1024), gate, x2d, w_o, ln_g, ln_b, 512)
    out_p = (k32.reshape(n_seq, seq, N_HEADS, HEAD_DIM), v32.reshape(n_seq, seq, N_HEADS, HEAD_DIM),
             logf.reshape(n_seq, seq, N_HEADS))

    rows = n_b * n_new
    xs2d = xs.reshape(rows, 1024)
    qs, ks32, _, vs32, gs, lfs, _ = _fox_proj(xs2d, w_main, w_f, b_fp, 1, rows)

    def heads_major(a, pad_to):
        a = a.reshape(n_b, n_new, N_HEADS, HEAD_DIM).transpose(0, 2, 1, 3)
        return jnp.pad(a, ((0, 0), (0, 0), (0, pad_to - n_new), (0, 0)))

    lfn = jnp.pad(lfs.reshape(n_b, n_new, N_HEADS).transpose(0, 2, 1),
                  ((0, 0), (0, 0), (0, LANES - n_new)))
    od = _fox_decode(page_table, heads_major(qs, 8), heads_major(ks32, LANES), heads_major(vs32, LANES),
                     lfn, cache_k, cache_v, cache_lft, layer)
    os2d = od[:, :, :n_new, :].transpose(0, 2, 1, 3).reshape(rows, 1024)
    ys = _fox_out(os2d, gs, xs2d, w_o, ln_g, ln_b, rows)
    out_s = (ks32.reshape(n_b, n_new, N_HEADS, HEAD_DIM), vs32.reshape(n_b, n_new, N_HEADS, HEAD_DIM),
             lfs.reshape(n_b, n_new, N_HEADS))
    return yp.reshape(n_seq, seq, 1024), ys.reshape(n_b, n_new, 1024), out_p, out_s


def _pool_layer(xp, xs, hist, w_in, w_grp, b_grp, scale, w_out, ln_g, ln_b):
    n_seq, seq, _ = xp.shape
    n_b, n_new, _ = xs.shape
    weights = (w_in.astype(BF16), w_grp.astype(BF16), _row(b_grp), _row(scale), w_out.astype(BF16),
               ln_g, ln_b)
    yp, hp = _pool_prompt(xp.reshape(n_seq * seq, 1024), *weights, n_seq, 256)
    ys, hs = _pool_sample(xs.transpose(1, 0, 2), hist.transpose(1, 0, 2), *weights)
    return (yp.reshape(n_seq, seq, 1024), ys.transpose(1, 0, 2),
            hp[:, HIST_PAD - POOL_HIST:, :], hs.transpose(1, 0, 2))


def _gmlp_layer(xp, xs, w_in, vg, vb, w_s, b_s, w_out, ln_g, ln_b):
    n_seq, seq, _ = xp.shape
    n_b, n_new, _ = xs.shape
    rows = n_b * n_new
    shared = (w_in.astype(BF16), _row(vg), _row(vb))
    tail = (w_out.astype(BF16), ln_g, ln_b)
    bs_p = jnp.repeat(b_s.T, GROUP_C, axis=1)
    yp, = _gmlp(xp.reshape(n_seq * seq, 1024), *shared, w_s, bs_p, *tail, 256, CHUNK, False)
    reps = CHUNK // n_new
    ws_s = jnp.tile(w_s[:, :n_new, :n_new], (1, reps, reps))
    bs_s = jnp.tile(jnp.repeat(b_s[:, :n_new].T, GROUP_C, axis=1), (reps, 1))
    ys, vrows = _gmlp(xs.reshape(rows, 1024), *shared, ws_s, bs_s, *tail, rows, n_new, True)
    return yp.reshape(n_seq, seq, 1024), ys.reshape(n_b, n_new, 1024), vrows.reshape(n_b, n_new, WIDTH_C)


def kernel(x_prompt, x_sample, cache_k, cache_v, cache_logf, state_pool, page_table, ln_g, ln_b, w_in_a, b_f_a, w_out_a, w_in_b, w_grp_b, b_grp_b, scale_b, w_out_b, w_in_c, ln_v_g, ln_v_b, w_s_c, b_s_c, w_out_c):
    n_layers_a, n_pool, page_size = cache_k.shape[:3]
    ck = cache_k.reshape(n_layers_a, n_pool, page_size * N_HEADS, HEAD_DIM)
    cv = cache_v.reshape(n_layers_a, n_pool, page_size * N_HEADS, HEAD_DIM)
    clft = jnp.swapaxes(cache_logf, 2, 3)
    xp, xs = x_prompt, x_sample
    kp_l, vp_l, fp_l, ks_l, vs_l, fs_l = [], [], [], [], [], []
    poolp_l, pools_l, chunkv_l = [], [], []
    for i in range(DEPTH):
        j = i // N_MIXERS
        kind = i % N_MIXERS
        g, b = _row(ln_g[i]), _row(ln_b[i])
        if kind == 0:
            xp, xs, out_p, out_s = _fox_layer(xp, xs, ck, cv, clft, page_table, j,
                                              w_in_a[j], b_f_a[j], w_out_a[j], g, b)
            kp_l.append(out_p[0]); vp_l.append(out_p[1]); fp_l.append(out_p[2])
            ks_l.append(out_s[0]); vs_l.append(out_s[1]); fs_l.append(out_s[2])
        elif kind == 1:
            xp, xs, hp, hs = _pool_layer(xp, xs, state_pool[j], w_in_b[j], w_grp_b[j], b_grp_b[j],
                                         scale_b[j], w_out_b[j], g, b)
            poolp_l.append(hp); pools_l.append(hs)
        else:
            xp, xs, vrows = _gmlp_layer(xp, xs, w_in_c[j], ln_v_g[j], ln_v_b[j], w_s_c[j], b_s_c[j],
                                        w_out_c[j], g, b)
            chunkv_l.append(vrows)
    return (xp, xs,
            jnp.stack(kp_l), jnp.stack(vp_l), jnp.stack(fp_l),
            jnp.stack(ks_l), jnp.stack(vs_l), jnp.stack(fs_l),
            jnp.stack(poolp_l), jnp.stack(pools_l),
            jnp.stack(chunkv_l))
```

```python
import functools

import jax
import jax.numpy as jnp
from jax import lax
from jax.experimental import pallas as pl
from jax.experimental.pallas import tpu as pltpu

D_MODEL = 1024
DEPTH = 4
N_MIXERS = 3
N_HEADS = 16
HEAD_DIM = 64
ATTN_SCALE = HEAD_DIM ** -0.5
WIDTH_B = 2048
POOL_WINDOWS = (2, 4, 8, 16)
GROUP_B = 512
POOL_HIST = 15
WIDTH_C = 2048
CHUNK = 128
N_GROUPS_C = 4
GROUP_C = 512
DN_ALPHA = (2 * DEPTH) ** 0.25
LN_EPS = 1e-5
LOG2E = 1.4426950408889634

LANES = 128
HIST_PAD = 16
MASK_VALUE = -1e30
VMEM_LIMIT_BYTES = 56 * 1024 * 1024

F32 = jnp.float32
BF16 = jnp.bfloat16


def _params(*sem):
    return pltpu.CompilerParams(dimension_semantics=sem, vmem_limit_bytes=VMEM_LIMIT_BYTES)


def _dot(a, b):
    return jnp.dot(a, b, preferred_element_type=F32)


def _dot_nt(a, b):
    return lax.dot_general(a, b, (((1,), (1,)), ((), ())), preferred_element_type=F32)


def _silu(x):
    return x / (1.0 + jnp.exp(-x))


def _log_sigmoid(f):
    return jnp.minimum(f, 0.0) - jnp.log1p(jnp.exp(-jnp.abs(f)))


def _layer_norm(z, g, b):
    mu = jnp.mean(z, axis=-1, keepdims=True)
    d = z - mu
    var = jnp.mean(d * d, axis=-1, keepdims=True)
    return d * lax.rsqrt(var + LN_EPS) * g + b


def _split3(x):
    hi = x.astype(BF16)
    r1 = x - hi.astype(F32)
    mid = r1.astype(BF16)
    lo = (r1 - mid.astype(F32)).astype(BF16)
    return hi, mid, lo


def _const_spec(shape):
    return pl.BlockSpec(shape, lambda *_: (0,) * len(shape))


def _fox_proj_prompt_kernel(x_ref, wt_ref, wk_ref, wg_ref, wf_ref, bf_ref, tri_ref,
                            qt_ref, kt_ref, vt_ref, kbf_ref, g_ref, logf_ref, caug_ref, carry_ref):
    @pl.when(pl.program_id(1) == 0)
    def _():
        carry_ref[...] = jnp.zeros_like(carry_ref)

    xb = x_ref[...].astype(BF16)
    qt_ref[0] = (_dot_nt(wt_ref[0:1024, :], xb) * (ATTN_SCALE * LOG2E)).astype(BF16)
    kt_ref[0] = _dot_nt(wt_ref[1024:2048, :], xb)
    vt_ref[0] = _dot_nt(wt_ref[2048:3072, :], xb)
    kbf_ref[...] = _dot(xb, wk_ref[...]).astype(BF16)
    g_ref[...] = _dot(xb, wg_ref[...])

    lf = _log_sigmoid(_dot(xb, wf_ref[...]) + bf_ref[...])
    logf_ref[...] = lf[:, :N_HEADS]
    hi, mid, lo = _split3(lf)
    tri = tri_ref[...]
    c = _dot(tri, hi) + _dot(tri, mid) + _dot(tri, lo) + carry_ref[...]
    carry_ref[...] = c[c.shape[0] - 1:, :]
    neg = -LOG2E * c
    nhi = neg.astype(BF16).astype(F32)
    rem = neg - nhi
    nmid = rem.astype(BF16).astype(F32)
    lane = lax.broadcasted_iota(jnp.int32, c.shape, 1)
    terms = jnp.where(lane < N_HEADS, nhi,
                      jnp.where(lane < 2 * N_HEADS, nmid,
                                jnp.where(lane < 3 * N_HEADS, rem - nmid, 0.0)))
    caug_ref[...] = terms.astype(BF16)


def _fox_proj_prompt(x2d, w_t, w_k, w_g, w_f, b_f, n_seq, tm):
    m = x2d.shape[0]
    seq = m // n_seq
    nt = seq // tm
    tri = jnp.tril(jnp.ones((tm, tm), BF16))
    row = lambda b, i: (b * nt + i, 0)
    wide = pl.BlockSpec((tm, 1024), row)
    tall = pl.BlockSpec((1, 1024, tm), lambda b, i: (b, 0, i))
    return pl.pallas_call(
        _fox_proj_prompt_kernel,
        grid=(n_seq, nt),
        in_specs=[wide, _const_spec((3072, 1024)), _const_spec((1024, 1024)), _const_spec((1024, 1024)),
                  _const_spec((1024, LANES)), _const_spec((1, LANES)), _const_spec((tm, tm))],
        out_specs=[tall, tall, tall, wide, wide, pl.BlockSpec((tm, N_HEADS), row),
                   pl.BlockSpec((tm, LANES), row)],
        out_shape=[jax.ShapeDtypeStruct((n_seq, 1024, seq), BF16),
                   jax.ShapeDtypeStruct((n_seq, 1024, seq), F32),
                   jax.ShapeDtypeStruct((n_seq, 1024, seq), F32),
                   jax.ShapeDtypeStruct((m, 1024), BF16),
                   jax.ShapeDtypeStruct((m, 1024), F32),
                   jax.ShapeDtypeStruct((m, N_HEADS), F32),
                   jax.ShapeDtypeStruct((m, LANES), BF16)],
        scratch_shapes=[pltpu.VMEM((1, LANES), F32)],
        compiler_params=_params("arbitrary", "arbitrary"),
        name="fox_proj_prompt",
    )(x2d, w_t, w_k, w_g, w_f, b_f, tri)


def _fox_proj_rows_kernel(x_ref, w_ref, wf_ref, bf_ref, q_ref, k_ref, v_ref, g_ref, logf_ref):
    xb = x_ref[...].astype(BF16)
    q_ref[...] = (_dot(xb, w_ref[:, 0:1024]) * (ATTN_SCALE * LOG2E)).astype(BF16)
    k_ref[...] = _dot(xb, w_ref[:, 1024:2048])
    v_ref[...] = _dot(xb, w_ref[:, 2048:3072])
    g_ref[...] = _dot(xb, w_ref[:, 3072:4096])
    lf = _log_sigmoid(_dot(xb, wf_ref[...]) + bf_ref[...])
    logf_ref[...] = lf[:, :N_HEADS]


def _fox_proj_rows(x2d, w_main, w_f, b_f):
    m = x2d.shape[0]
    wide = _const_spec((m, 1024))
    return pl.pallas_call(
        _fox_proj_rows_kernel,
        grid=(1,),
        in_specs=[wide, _const_spec((1024, 4096)), _const_spec((1024, LANES)), _const_spec((1, LANES))],
        out_specs=[wide, wide, wide, wide, _const_spec((m, N_HEADS))],
        out_shape=[jax.ShapeDtypeStruct((m, 1024), BF16),
                   jax.ShapeDtypeStruct((m, 1024), F32),
                   jax.ShapeDtypeStruct((m, 1024), F32),
                   jax.ShapeDtypeStruct((m, 1024), F32),
                   jax.ShapeDtypeStruct((m, N_HEADS), F32)],
        compiler_params=_params("arbitrary"),
        name="fox_proj_rows",
    )(x2d, w_main, w_f, b_f)


def _fox_attn_kernel(qt_ref, k_ref, caug_ref, vt_ref, o_ref,
                     w_ref, sa_ref, sb_ref, m_ref, l_ref, acc_ref, *, t):
    pair = pl.program_id(1)
    qi = pl.program_id(2)
    qt2 = qt_ref[0]
    row = lax.broadcasted_iota(jnp.int32, (LANES, t), 0)
    z64 = jnp.zeros((HEAD_DIM, t), BF16)
    for h in range(2):
        head = 2 * pair + h
        sel = (row == head) | (row == head + N_HEADS) | (row == head + 2 * N_HEADS)
        sel = jnp.where(sel, 1.0, 0.0).astype(BF16)
        top = [qt2[0:HEAD_DIM], z64] if h == 0 else [z64, qt2[HEAD_DIM:]]
        w_ref[h] = jnp.concatenate(top + [sel], axis=0)
    m_ref[...] = jnp.full_like(m_ref, MASK_VALUE)
    l_ref[...] = jnp.zeros_like(l_ref)
    acc_ref[...] = jnp.zeros_like(acc_ref)

    def scores(kt, dst_ref, masked):
        ks = pl.multiple_of(kt * t, t)
        kaug = jnp.concatenate([k_ref[0, pl.ds(ks, t), :], caug_ref[0, pl.ds(ks, t), :]], axis=1)
        for h in range(2):
            s = _dot(kaug, w_ref[h])
            if masked:
                kidx = lax.broadcasted_iota(jnp.int32, (t, t), 0)
                qidx = lax.broadcasted_iota(jnp.int32, (t, t), 1)
                s = jnp.where(kidx <= qidx, s, MASK_VALUE)
            dst_ref[h] = s

    def update(kt, src_ref):
        ks = pl.multiple_of(kt * t, t)
        for h in range(2):
            s = src_ref[h]
            vt = vt_ref[0, h * HEAD_DIM:(h + 1) * HEAD_DIM, pl.ds(ks, t)].astype(BF16)
            m_prev = m_ref[h]
            m_new = jnp.maximum(m_prev, jnp.max(s, axis=0, keepdims=True))
            alpha = jnp.exp2(m_prev - m_new)
            p = jnp.exp2(s - m_new)
            l_ref[h] = alpha * l_ref[h] + jnp.sum(p, axis=0, keepdims=True)
            acc_ref[h] = alpha * acc_ref[h] + _dot(vt, p.astype(BF16))
            m_ref[h] = m_new

    def step(kt, masked_next):
        even = kt % 2 == 0

        @pl.when(even)
        def _():
            scores(kt + 1, sb_ref, masked_next)
            update(kt, sa_ref)

        @pl.when(jnp.logical_not(even))
        def _():
            scores(kt + 1, sa_ref, masked_next)
            update(kt, sb_ref)

    @pl.when(qi == 0)
    def _():
        scores(0, sa_ref, True)

    @pl.when(qi > 0)
    def _():
        scores(0, sa_ref, False)

        def body(kt, carry):
            step(kt, False)
            return carry

        lax.fori_loop(0, qi - 1, body, 0)
        step(qi - 1, True)

    @pl.when(qi % 2 == 0)
    def _():
        update(qi, sa_ref)

    @pl.when(qi % 2 == 1)
    def _():
        update(qi, sb_ref)

    o2 = jnp.concatenate([acc_ref[0] / l_ref[0], acc_ref[1] / l_ref[1]], axis=0)
    o_ref[0] = o2.T


def _fox_attn(qt, kbf, caug, vt, t):
    n_seq, _, seq = qt.shape
    return pl.pallas_call(
        functools.partial(_fox_attn_kernel, t=t),
        grid=(n_seq, N_HEADS // 2, seq // t),
        in_specs=[pl.BlockSpec((1, LANES, t), lambda b, p, i: (b, p, i)),
                  pl.BlockSpec((1, seq, LANES), lambda b, p, i: (b, 0, p)),
                  pl.BlockSpec((1, seq, LANES), lambda b, p, i: (b, 0, 0)),
                  pl.BlockSpec((1, LANES, seq), lambda b, p, i: (b, p, 0))],
        out_specs=pl.BlockSpec((1, t, LANES), lambda b, p, i: (b, i, p)),
        out_shape=jax.ShapeDtypeStruct((n_seq, seq, 1024), F32),
        scratch_shapes=[pltpu.VMEM((2, 2 * LANES, t), BF16),
                        pltpu.VMEM((2, t, t), F32),
                        pltpu.VMEM((2, t, t), F32),
                        pltpu.VMEM((2, 1, t), F32),
                        pltpu.VMEM((2, 1, t), F32),
                        pltpu.VMEM((2, HEAD_DIM, t), F32)],
        compiler_params=_params("arbitrary", "arbitrary", "arbitrary"),
        name="fox_attn",
    )(qt, kbf, caug, vt)


def _fox_decode_kernel(pt_ref, q_ref, kn_ref, vn_ref, lfn_ref, tri_ref, *rest, n_new, group):
    del pt_ref
    k_refs, v_refs, lf_refs = rest[:group], rest[group:2 * group], rest[2 * group:3 * group]
    o_ref, qbd_ref, kbf_ref, vbf_ref, m_ref, l_ref, acc_ref, r_ref = rest[3 * group:]
    step = pl.program_id(1)
    n_rows = n_new * N_HEADS
    row = lax.broadcasted_iota(jnp.int32, (n_rows, 1024), 0)
    lane = lax.broadcasted_iota(jnp.int32, (n_rows, 1024), 1)
    own_head = (lane >> 6) == (row & (N_HEADS - 1))

    def lane_sums(lf, tri):
        out = _dot(jnp.concatenate(_split3(lf), axis=0), tri)
        return out[0:N_HEADS] + out[N_HEADS:2 * N_HEADS] + out[2 * N_HEADS:]

    def widen(x, reps):
        return jnp.concatenate([x] * reps, axis=1)

    @pl.when(step == 0)
    def _():
        q = q_ref[0].astype(F32)
        qb = jnp.concatenate([jnp.broadcast_to(q[t:t + 1], (N_HEADS, 1024)) for t in range(n_new)], axis=0)
        qb = jnp.where(own_head, qb, 0.0)
        qbd_ref[...] = qb.astype(BF16)
        rr = lax.broadcasted_iota(jnp.int32, (LANES, LANES), 0)
        cc = lax.broadcasted_iota(jnp.int32, (LANES, LANES), 1)
        incl = jnp.where(rr <= cc, 1.0, 0.0).astype(BF16)
        bias = -LOG2E * lane_sums(lfn_ref[0], incl)
        bias = jnp.concatenate([bias] * n_new, axis=0)
        j_idx = lax.broadcasted_iota(jnp.int32, (n_rows, LANES), 1)
        t_idx = lax.broadcasted_iota(jnp.int32, (n_rows, LANES), 0) >> 4
        kn = kn_ref[0]
        vn = vn_ref[0]
        s = jnp.full((n_rows, LANES), MASK_VALUE, F32)
        for j in range(n_new):
            sj = jnp.sum(qb * kn[j:j + 1, :], axis=-1, keepdims=True)
            s = jnp.where(j_idx == j, sj, s)
        s = jnp.where(j_idx <= t_idx, s + bias, MASK_VALUE)
        m = jnp.max(s, axis=-1, keepdims=True)
        p = jnp.exp2(s - m)
        m_ref[...] = jnp.broadcast_to(m, (n_rows, LANES))
        l_ref[...] = jnp.broadcast_to(jnp.sum(p, axis=-1, keepdims=True), (n_rows, LANES))
        acc = jnp.zeros((n_rows, 1024), F32)
        for j in range(n_new):
            acc = acc + p[:, j:j + 1] * vn[j:j + 1, :]
        acc_ref[...] = acc
        r_ref[...] = jnp.zeros_like(r_ref)

    r = r_ref[...]
    biases = []
    for i in range(group):
        kbf_ref[:, i * LANES:(i + 1) * LANES] = k_refs[i][0, 0].astype(BF16)
        vbf_ref[:, i * LANES:(i + 1) * LANES] = v_refs[i][0, 0].astype(BF16)
        lf = lf_refs[i][0, 0]
        biases.append(lane_sums(lf, tri_ref[...]) + r)
        r = r + jnp.sum(lf, axis=-1, keepdims=True)
    r_ref[...] = r
    bias = LOG2E * jnp.concatenate(biases, axis=1)
    s = _dot(qbd_ref[...], kbf_ref[...]) + jnp.concatenate([bias] * n_new, axis=0)
    m_prev = m_ref[...]
    m_new = jnp.maximum(m_prev, jnp.max(s, axis=-1, keepdims=True))
    alpha = jnp.exp2(m_prev - m_new)
    p = jnp.exp2(s - widen(m_new, group))
    l_ref[...] = alpha * l_ref[...] + jnp.sum(p, axis=-1, keepdims=True)
    acc_ref[...] = widen(alpha, 1024 // LANES) * acc_ref[...] + _dot_nt(p.astype(BF16), vbf_ref[...])
    m_ref[...] = m_new

    @pl.when(step == pl.num_programs(1) - 1)
    def _():
        o = jnp.where(own_head, acc_ref[...] / widen(l_ref[...], 1024 // LANES), 0.0)
        for t in range(n_new):
            o_ref[0, t:t + 1, :] = jnp.sum(o[t * N_HEADS:(t + 1) * N_HEADS], axis=0, keepdims=True)


def _fox_decode(page_table, q, kn, vn, lfn, cache_kt, cache_vt, cache_lft, layer, group):
    n_b, n_pages = page_table.shape
    n_new = q.shape[1]
    rr = lax.broadcasted_iota(jnp.int32, (LANES, LANES), 0)
    cc = lax.broadcasted_iota(jnp.int32, (LANES, LANES), 1)
    strict = jnp.where(rr > cc, 1.0, 0.0).astype(BF16)

    def page_spec(shape, i):
        return pl.BlockSpec((1, 1) + shape,
                            lambda b, s, pt: (layer, pt[b, n_pages - 1 - (s * group + i)], 0, 0))

    per_b = lambda shape: pl.BlockSpec((1,) + shape, lambda b, s, pt: (b,) + (0,) * len(shape))
    n_rows = n_new * N_HEADS
    grid_spec = pltpu.PrefetchScalarGridSpec(
        num_scalar_prefetch=1,
        grid=(n_b, n_pages // group),
        in_specs=([per_b((n_new, 1024)), per_b((n_new, 1024)), per_b((n_new, 1024)), per_b((N_HEADS, LANES)),
                   pl.BlockSpec((LANES, LANES), lambda b, s, pt: (0, 0))]
                  + [page_spec((1024, LANES), i) for i in range(group)]
                  + [page_spec((1024, LANES), i) for i in range(group)]
                  + [page_spec((N_HEADS, LANES), i) for i in range(group)]),
        out_specs=per_b((n_new, 1024)),
        scratch_shapes=[pltpu.VMEM((n_rows, 1024), BF16),
                        pltpu.VMEM((1024, LANES * group), BF16),
                        pltpu.VMEM((1024, LANES * group), BF16),
                        pltpu.VMEM((n_rows, LANES), F32),
                        pltpu.VMEM((n_rows, LANES), F32),
                        pltpu.VMEM((n_rows, 1024), F32),
                        pltpu.VMEM((N_HEADS, LANES), F32)])
    return pl.pallas_call(
        functools.partial(_fox_decode_kernel, n_new=n_new, group=group),
        grid_spec=grid_spec,
        out_shape=jax.ShapeDtypeStruct((n_b, n_new, 1024), F32),
        compiler_params=_params("arbitrary", "arbitrary"),
        name="fox_decode",
    )(page_table, q, kn, vn, lfn, strict,
      *([cache_kt] * group), *([cache_vt] * group), *([cache_lft] * group))


def _fox_out_kernel(o_ref, g_ref, x_ref, w_ref, lng_ref, lnb_ref, y_ref):
    a = (o_ref[...] * _silu(g_ref[...])).astype(BF16)
    z = DN_ALPHA * x_ref[...] + _dot(a, w_ref[...])
    y_ref[...] = _layer_norm(z, lng_ref[...], lnb_ref[...])


def _fox_out(o2d, g2d, x2d, w_out, ln_g, ln_b, tm):
    m = x2d.shape[0]
    wide = pl.BlockSpec((tm, 1024), lambda i: (i, 0))
    return pl.pallas_call(
        _fox_out_kernel,
        grid=(m // tm,),
        in_specs=[wide, wide, wide, _const_spec((1024, 1024)), _const_spec((1, 1024)),
                  _const_spec((1, 1024))],
        out_specs=wide,
        out_shape=jax.ShapeDtypeStruct((m, 1024), F32),
        compiler_params=_params("arbitrary"),
        name="fox_out",
    )(o2d, g2d, x2d, w_out, ln_g, ln_b)


def _pool_tail(m_bf, gate, x, wg_ref, bg_ref, sc_ref, wo_ref, lng_ref, lnb_ref):
    z = jnp.concatenate([_dot(m_bf[:, g * GROUP_B:(g + 1) * GROUP_B], wg_ref[g])
                         for g in range(len(POOL_WINDOWS))], axis=1) + bg_ref[...]
    a = (z * sc_ref[...] * _silu(gate)).astype(BF16)
    zz = DN_ALPHA * x + _dot(a, wo_ref[...])
    return _layer_norm(zz, lng_ref[...], lnb_ref[...])


def _pool_prompt_kernel(x_ref, wi_ref, wg_ref, bg_ref, sc_ref, wo_ref, lng_ref, lnb_ref,
                        y_ref, hist_ref, ext_ref, m_ref, *, tm):
    i = pl.program_id(1)

    @pl.when(i == 0)
    def _():
        ext_ref[0:HIST_PAD, :] = jnp.zeros((HIST_PAD, WIDTH_B), F32)

    x = x_ref[...]
    xb = x.astype(BF16)
    a = _dot(xb, wi_ref[:, 0:WIDTH_B])
    ext_ref[HIST_PAD:, :] = a
    for g, w in enumerate(POOL_WINDOWS):
        cols = slice(g * GROUP_B, (g + 1) * GROUP_B)
        win = a[:, cols]
        for d in range(1, w):
            win = win + ext_ref[HIST_PAD - d:HIST_PAD - d + tm, cols]
        m_ref[:, cols] = (win * (1.0 / w) - a[:, cols]).astype(BF16)

    @pl.when(i == 0)
    def _():
        pos = lax.broadcasted_iota(jnp.int32, (HIST_PAD, GROUP_B), 0)
        for g, w in enumerate(POOL_WINDOWS):
            cols = slice(g * GROUP_B, (g + 1) * GROUP_B)
            win = ext_ref[HIST_PAD:2 * HIST_PAD, cols]
            for d in range(1, w):
                win = win + ext_ref[HIST_PAD - d:2 * HIST_PAD - d, cols]
            cnt = jnp.minimum(pos + 1, w).astype(F32)
            m_ref[0:HIST_PAD, cols] = (win / cnt - ext_ref[HIST_PAD:2 * HIST_PAD, cols]).astype(BF16)

    gate = _dot(xb, wi_ref[:, WIDTH_B:])
    y_ref[...] = _pool_tail(m_ref[...], gate, x, wg_ref, bg_ref, sc_ref, wo_ref, lng_ref, lnb_ref)
    hist_ref[0] = a[tm - HIST_PAD:, :]
    ext_ref[0:HIST_PAD, :] = a[tm - HIST_PAD:, :]


def _pool_prompt(x2d, w_in, w_grp, b_grp, scale, w_out, ln_g, ln_b, n_seq, tm):
    m = x2d.shape[0]
    nt = m // (n_seq * tm)
    wide = pl.BlockSpec((tm, 1024), lambda b, i: (b * nt + i, 0))
    return pl.pallas_call(
        functools.partial(_pool_prompt_kernel, tm=tm),
        grid=(n_seq, nt),
        in_specs=[wide, _const_spec((1024, 2 * WIDTH_B)), _const_spec((4, GROUP_B, GROUP_B)),
                  _const_spec((1, WIDTH_B)), _const_spec((1, WIDTH_B)), _const_spec((WIDTH_B, 1024)),
                  _const_spec((1, 1024)), _const_spec((1, 1024))],
        out_specs=[wide, pl.BlockSpec((1, HIST_PAD, WIDTH_B), lambda b, i: (b, 0, 0))],
        out_shape=[jax.ShapeDtypeStruct((m, 1024), F32),
                   jax.ShapeDtypeStruct((n_seq, HIST_PAD, WIDTH_B), F32)],
        scratch_shapes=[pltpu.VMEM((HIST_PAD + tm, WIDTH_B), F32),
                        pltpu.VMEM((tm, WIDTH_B), BF16)],
        compiler_params=_params("arbitrary", "arbitrary"),
        name="pool_prompt",
    )(x2d, w_in, w_grp, b_grp, scale, w_out, ln_g, ln_b)


def _pool_sample_kernel(x_ref, hist_ref, wi_ref, wg_ref, bg_ref, sc_ref, wo_ref, lng_ref, lnb_ref,
                        y_ref, newhist_ref, *, n_new):
    xs = [x_ref[t] for t in range(n_new)]
    a = [_dot(x.astype(BF16), wi_ref[:, 0:WIDTH_B]) for x in xs]
    ext = [hist_ref[r] for r in range(POOL_HIST)] + a
    for r in range(POOL_HIST):
        newhist_ref[r] = ext[n_new + r]
    for t in range(n_new):
        parts = []
        for g, w in enumerate(POOL_WINDOWS):
            cols = slice(g * GROUP_B, (g + 1) * GROUP_B)
            win = a[t][:, cols]
            for d in range(1, w):
                win = win + ext[POOL_HIST + t - d][:, cols]
            parts.append(win * (1.0 / w) - a[t][:, cols])
        m_bf = jnp.concatenate(parts, axis=1).astype(BF16)
        gate = _dot(xs[t].astype(BF16), wi_ref[:, WIDTH_B:])
        y_ref[t] = _pool_tail(m_bf, gate, xs[t], wg_ref, bg_ref, sc_ref, wo_ref, lng_ref, lnb_ref)


def _pool_sample(x_tm, hist_tm, w_in, w_grp, b_grp, scale, w_out, ln_g, ln_b):
    n_new, n_b, _ = x_tm.shape
    full = lambda a: _const_spec(a.shape)
    args = (x_tm, hist_tm, w_in, w_grp, b_grp, scale, w_out, ln_g, ln_b)
    return pl.pallas_call(
        functools.partial(_pool_sample_kernel, n_new=n_new),
        grid=(1,),
        in_specs=[full(a) for a in args],
        out_specs=[_const_spec((n_new, n_b, 1024)), _const_spec((POOL_HIST, n_b, WIDTH_B))],
        out_shape=[jax.ShapeDtypeStruct((n_new, n_b, 1024), F32),
                   jax.ShapeDtypeStruct((POOL_HIST, n_b, WIDTH_B), F32)],
        compiler_params=_params("arbitrary"),
        name="pool_sample",
    )(*args)


def _gmlp_kernel(x_ref, wi_ref, vg_ref, vb_ref, ws_ref, bs_ref, wo_ref, lng_ref, lnb_ref,
                 *out_refs, tm, span, emit_v):
    y_ref = out_refs[0]
    x = x_ref[...]
    xb = x.astype(BF16)
    v = _layer_norm(_dot(xb, wi_ref[:, WIDTH_C:2 * WIDTH_C]), vg_ref[...], vb_ref[...])
    if emit_v:
        out_refs[1][...] = v
    vb16 = v.astype(BF16)
    r = lax.broadcasted_iota(jnp.int32, (CHUNK, CHUNK), 0)
    c = lax.broadcasted_iota(jnp.int32, (CHUNK, CHUNK), 1)
    keep = (c <= r) & ((r ^ c) < span)
    chunks = []
    for n in range(tm // CHUNK):
        rows = slice(n * CHUNK, (n + 1) * CHUNK)
        groups = []
        for g in range(N_GROUPS_C):
            wmix = jnp.where(keep, ws_ref[g], 0.0).astype(BF16)
            groups.append(_dot(wmix, vb16[rows, g * GROUP_C:(g + 1) * GROUP_C]))
        chunks.append(jnp.concatenate(groups, axis=1) + bs_ref[...])
    sv = jnp.concatenate(chunks, axis=0) if len(chunks) > 1 else chunks[0]
    u = _dot(xb, wi_ref[:, 0:WIDTH_C])
    gate = _dot(xb, wi_ref[:, 2 * WIDTH_C:])
    a = (u * sv * _silu(gate)).astype(BF16)
    z = DN_ALPHA * x + _dot(a, wo_ref[...])
    y_ref[...] = _layer_norm(z, lng_ref[...], lnb_ref[...])


def _gmlp(x2d, w_in, vg, vb, ws_tiled, bs_rows, w_out, ln_g, ln_b, tm, span, emit_v):
    m = x2d.shape[0]
    wide = pl.BlockSpec((tm, 1024), lambda i: (i, 0))
    out_specs = [wide]
    out_shape = [jax.ShapeDtypeStruct((m, 1024), F32)]
    if emit_v:
        out_specs.append(pl.BlockSpec((tm, WIDTH_C), lambda i: (i, 0)))
        out_shape.append(jax.ShapeDtypeStruct((m, WIDTH_C), F32))
    return pl.pallas_call(
        functools.partial(_gmlp_kernel, tm=tm, span=span, emit_v=emit_v),
        grid=(m // tm,),
        in_specs=[wide, _const_spec((1024, 3 * WIDTH_C)), _const_spec((1, WIDTH_C)),
                  _const_spec((1, WIDTH_C)), _const_spec((N_GROUPS_C, CHUNK, CHUNK)),
                  _const_spec((CHUNK, WIDTH_C)), _const_spec((WIDTH_C, 1024)),
                  _const_spec((1, 1024)), _const_spec((1, 1024))],
        out_specs=out_specs,
        out_shape=out_shape,
        compiler_params=_params("arbitrary"),
        name="gmlp_v" if emit_v else "gmlp",
    )(x2d, w_in, vg, vb, ws_tiled, bs_rows, w_out, ln_g, ln_b)


def _row(v):
    return v.reshape(1, -1)


def _fox_layer(xp, xs, cache_kt, cache_vt, cache_lft, page_table, layer, w_in, b_f, w_out, ln_g, ln_b):
    n_seq, seq, _ = xp.shape
    n_b, n_new, _ = xs.shape
    w_main = w_in[:, :4 * 1024].astype(BF16)
    w_f = jnp.pad(jnp.tile(w_in[:, 4 * 1024:], (1, 3)), ((0, 0), (0, LANES - 3 * N_HEADS))).astype(BF16)
    b_fp = jnp.pad(jnp.tile(b_f, 3), (0, LANES - 3 * N_HEADS)).reshape(1, LANES)
    w_o = w_out.astype(BF16)

    x2d = xp.reshape(n_seq * seq, 1024)
    qt, kt32, vt32, kbf, gate, logf, caug = _fox_proj_prompt(
        x2d, w_main[:, :3 * 1024].T, w_main[:, 1024:2048], w_main[:, 3 * 1024:], w_f, b_fp, n_seq, 512)
    o = _fox_attn(qt, kbf.reshape(n_seq, seq, 1024), caug.reshape(n_seq, seq, LANES), vt32, 512)
    yp = _fox_out(o.reshape(n_seq * seq, 1024), gate, x2d, w_o, ln_g, ln_b, 512)
    out_p = (kt32, vt32, logf.reshape(n_seq, seq, N_HEADS))

    rows = n_b * n_new
    xs2d = xs.reshape(rows, 1024)
    qs, ks32, vs32, gs, lfs = _fox_proj_rows(xs2d, w_main, w_f, b_fp)
    lfn = jnp.pad(lfs.reshape(n_b, n_new, N_HEADS).transpose(0, 2, 1),
                  ((0, 0), (0, 0), (0, LANES - n_new)))
    od = _fox_decode(page_table, qs.reshape(n_b, n_new, 1024), ks32.reshape(n_b, n_new, 1024),
                     vs32.reshape(n_b, n_new, 1024), lfn, cache_kt, cache_vt, cache_lft, layer, 8)
    ys = _fox_out(od.reshape(rows, 1024), gs, xs2d, w_o, ln_g, ln_b, rows)
    out_s = (ks32.reshape(n_b, n_new, N_HEADS, HEAD_DIM), vs32.reshape(n_b, n_new, N_HEADS, HEAD_DIM),
             lfs.reshape(n_b, n_new, N_HEADS))
    return yp.reshape(n_seq, seq, 1024), ys.reshape(n_b, n_new, 1024), out_p, out_s


def _pool_layer(xp, xs, hist, w_in, w_grp, b_grp, scale, w_out, ln_g, ln_b):
    n_seq, seq, _ = xp.shape
    n_b, n_new, _ = xs.shape
    weights = (w_in.astype(BF16), w_grp.astype(BF16), _row(b_grp), _row(scale), w_out.astype(BF16),
               ln_g, ln_b)
    yp, hp = _pool_prompt(xp.reshape(n_seq * seq, 1024), *weights, n_seq, 256)
    ys, hs = _pool_sample(xs.transpose(1, 0, 2), hist.transpose(1, 0, 2), *weights)
    return (yp.reshape(n_seq, seq, 1024), ys.transpose(1, 0, 2),
            hp[:, HIST_PAD - POOL_HIST:, :], hs.transpose(1, 0, 2))


def _gmlp_layer(xp, xs, w_in, vg, vb, w_s, b_s, w_out, ln_g, ln_b):
    n_seq, seq, _ = xp.shape
    n_b, n_new, _ = xs.shape
    rows = n_b * n_new
    shared = (w_in.astype(BF16), _row(vg), _row(vb))
    tail = (w_out.astype(BF16), ln_g, ln_b)
    bs_p = jnp.repeat(b_s.T, GROUP_C, axis=1)
    yp, = _gmlp(xp.reshape(n_seq * seq, 1024), *shared, w_s, bs_p, *tail, 256, CHUNK, False)
    reps = CHUNK // n_new
    ws_s = jnp.tile(w_s[:, :n_new, :n_new], (1, reps, reps))
    bs_s = jnp.tile(jnp.repeat(b_s[:, :n_new].T, GROUP_C, axis=1), (reps, 1))
    ys, vrows = _gmlp(xs.reshape(rows, 1024), *shared, ws_s, bs_s, *tail, rows, n_new, True)
    return yp.reshape(n_seq, seq, 1024), ys.reshape(n_b, n_new, 1024), vrows.reshape(n_b, n_new, WIDTH_C)


def _rows_major(t_l):
    t = jnp.stack(t_l)
    return t.reshape(t.shape[0], t.shape[1], N_HEADS, HEAD_DIM, t.shape[3]).transpose(0, 1, 4, 2, 3)


def kernel(x_prompt, x_sample, cache_k, cache_v, cache_logf, state_pool, page_table, ln_g, ln_b, w_in_a, b_f_a, w_out_a, w_in_b, w_grp_b, b_grp_b, scale_b, w_out_b, w_in_c, ln_v_g, ln_v_b, w_s_c, b_s_c, w_out_c):
    n_layers_a, n_pool, page_size = cache_k.shape[:3]
    ck = cache_k.transpose(0, 1, 3, 4, 2).reshape(n_layers_a, n_pool, N_HEADS * HEAD_DIM, page_size)
    cv = cache_v.transpose(0, 1, 3, 4, 2).reshape(n_layers_a, n_pool, N_HEADS * HEAD_DIM, page_size)
    clft = jnp.swapaxes(cache_logf, 2, 3)
    xp, xs = x_prompt, x_sample
    kp_l, vp_l, fp_l, ks_l, vs_l, fs_l = [], [], [], [], [], []
    poolp_l, pools_l, chunkv_l = [], [], []
    for i in range(DEPTH):
        j = i // N_MIXERS
        kind = i % N_MIXERS
        g, b = _row(ln_g[i]), _row(ln_b[i])
        if kind == 0:
            xp, xs, out_p, out_s = _fox_layer(xp, xs, ck, cv, clft, page_table, j,
                                              w_in_a[j], b_f_a[j], w_out_a[j], g, b)
            kp_l.append(out_p[0]); vp_l.append(out_p[1]); fp_l.append(out_p[2])
            ks_l.append(out_s[0]); vs_l.append(out_s[1]); fs_l.append(out_s[2])
        elif kind == 1:
            xp, xs, hp, hs = _pool_layer(xp, xs, state_pool[j], w_in_b[j], w_grp_b[j], b_grp_b[j],
                                         scale_b[j], w_out_b[j], g, b)
            poolp_l.append(hp); pools_l.append(hs)
        else:
            xp, xs, vrows = _gmlp_layer(xp, xs, w_in_c[j], ln_v_g[j], ln_v_b[j], w_s_c[j], b_s_c[j],
                                        w_out_c[j], g, b)
            chunkv_l.append(vrows)
    return (xp, xs,
            _rows_major(kp_l), _rows_major(vp_l), jnp.stack(fp_l),
            jnp.stack(ks_l), jnp.stack(vs_l), jnp.stack(fs_l),
            jnp.stack(poolp_l), jnp.stack(pools_l),
            jnp.stack(chunkv_l))
```

```python
import functools

import jax
import jax.numpy as jnp
from jax import lax
from jax.experimental import pallas as pl
from jax.experimental.pallas import tpu as pltpu

D_MODEL = 1024
DEPTH = 4
N_MIXERS = 3
N_HEADS = 16
HEAD_DIM = 64
ATTN_SCALE = HEAD_DIM ** -0.5
WIDTH_B = 2048
POOL_WINDOWS = (2, 4, 8, 16)
GROUP_B = 512
POOL_HIST = 15
WIDTH_C = 2048
CHUNK = 128
N_GROUPS_C = 4
GROUP_C = 512
DN_ALPHA = (2 * DEPTH) ** 0.25
LN_EPS = 1e-5
LOG2E = 1.4426950408889634

LANES = 128
HIST_PAD = 16
ONES_ROWS = 16
MASK_VALUE = -1e30
VMEM_LIMIT_BYTES = 56 * 1024 * 1024

F32 = jnp.float32
BF16 = jnp.bfloat16


def _params(*sem):
    return pltpu.CompilerParams(dimension_semantics=sem, vmem_limit_bytes=VMEM_LIMIT_BYTES)


def _dot(a, b):
    return jnp.dot(a, b, preferred_element_type=F32)


def _dot_nt(a, b):
    return lax.dot_general(a, b, (((1,), (1,)), ((), ())), preferred_element_type=F32)


def _silu(x):
    return x / (1.0 + jnp.exp(-x))


def _log_sigmoid(f):
    return jnp.minimum(f, 0.0) - jnp.log1p(jnp.exp(-jnp.abs(f)))


def _layer_norm(z, g, b):
    mu = jnp.mean(z, axis=-1, keepdims=True)
    d = z - mu
    var = jnp.mean(d * d, axis=-1, keepdims=True)
    return d * lax.rsqrt(var + LN_EPS) * g + b


def _split3(x):
    hi = x.astype(BF16)
    r1 = x - hi.astype(F32)
    mid = r1.astype(BF16)
    lo = (r1 - mid.astype(F32)).astype(BF16)
    return hi, mid, lo


def _const_spec(shape):
    return pl.BlockSpec(shape, lambda *_: (0,) * len(shape))


def _fox_proj_prompt_kernel(x_ref, wt_ref, wk_ref, wg_ref, wf_ref, bf_ref, tri_ref,
                            qt_ref, kt_ref, vt_ref, kbf_ref, g_ref, logf_ref, caug_ref, carry_ref):
    @pl.when(pl.program_id(1) == 0)
    def _():
        carry_ref[...] = jnp.zeros_like(carry_ref)

    xb = x_ref[...].astype(BF16)
    qt_ref[0] = (_dot_nt(wt_ref[0:1024, :], xb) * (ATTN_SCALE * LOG2E)).astype(BF16)
    kt_ref[0] = _dot_nt(wt_ref[1024:2048, :], xb)
    vt_ref[0] = _dot_nt(wt_ref[2048:3072, :], xb)
    kbf_ref[...] = _dot(xb, wk_ref[...]).astype(BF16)
    g_ref[...] = _dot(xb, wg_ref[...])

    lf = _log_sigmoid(_dot(xb, wf_ref[...]) + bf_ref[...])
    logf_ref[...] = lf[:, :N_HEADS]
    hi, mid, lo = _split3(lf)
    tri = tri_ref[...]
    c = _dot(tri, hi) + _dot(tri, mid) + _dot(tri, lo) + carry_ref[...]
    carry_ref[...] = c[c.shape[0] - 1:, :]
    neg = -LOG2E * c
    nhi = neg.astype(BF16).astype(F32)
    rem = neg - nhi
    nmid = rem.astype(BF16).astype(F32)
    lane = lax.broadcasted_iota(jnp.int32, c.shape, 1)
    terms = jnp.where(lane < N_HEADS, nhi,
                      jnp.where(lane < 2 * N_HEADS, nmid,
                                jnp.where(lane < 3 * N_HEADS, rem - nmid, 0.0)))
    caug_ref[...] = terms.astype(BF16)


def _fox_proj_prompt(x2d, w_t, w_k, w_g, w_f, b_f, n_seq, tm):
    m = x2d.shape[0]
    seq = m // n_seq
    nt = seq // tm
    tri = jnp.tril(jnp.ones((tm, tm), BF16))
    row = lambda b, i: (b * nt + i, 0)
    wide = pl.BlockSpec((tm, 1024), row)
    tall = pl.BlockSpec((1, 1024, tm), lambda b, i: (b, 0, i))
    return pl.pallas_call(
        _fox_proj_prompt_kernel,
        grid=(n_seq, nt),
        in_specs=[wide, _const_spec((3072, 1024)), _const_spec((1024, 1024)), _const_spec((1024, 1024)),
                  _const_spec((1024, LANES)), _const_spec((1, LANES)), _const_spec((tm, tm))],
        out_specs=[tall, tall, tall, wide, wide, pl.BlockSpec((tm, N_HEADS), row),
                   pl.BlockSpec((tm, LANES), row)],
        out_shape=[jax.ShapeDtypeStruct((n_seq, 1024, seq), BF16),
                   jax.ShapeDtypeStruct((n_seq, 1024, seq), F32),
                   jax.ShapeDtypeStruct((n_seq, 1024, seq), F32),
                   jax.ShapeDtypeStruct((m, 1024), BF16),
                   jax.ShapeDtypeStruct((m, 1024), F32),
                   jax.ShapeDtypeStruct((m, N_HEADS), F32),
                   jax.ShapeDtypeStruct((m, LANES), BF16)],
        scratch_shapes=[pltpu.VMEM((1, LANES), F32)],
        compiler_params=_params("arbitrary", "arbitrary"),
        name="fox_proj_prompt",
    )(x2d, w_t, w_k, w_g, w_f, b_f, tri)


def _fox_proj_rows_kernel(x_ref, w_ref, wf_ref, bf_ref, q_ref, k_ref, v_ref, g_ref, logf_ref):
    xb = x_ref[...].astype(BF16)
    q_ref[...] = (_dot(xb, w_ref[:, 0:1024]) * (ATTN_SCALE * LOG2E)).astype(BF16)
    k_ref[...] = _dot(xb, w_ref[:, 1024:2048])
    v_ref[...] = _dot(xb, w_ref[:, 2048:3072])
    g_ref[...] = _dot(xb, w_ref[:, 3072:4096])
    lf = _log_sigmoid(_dot(xb, wf_ref[...]) + bf_ref[...])
    logf_ref[...] = lf[:, :N_HEADS]


def _fox_proj_rows(x2d, w_main, w_f, b_f):
    m = x2d.shape[0]
    wide = _const_spec((m, 1024))
    return pl.pallas_call(
        _fox_proj_rows_kernel,
        grid=(1,),
        in_specs=[wide, _const_spec((1024, 4096)), _const_spec((1024, LANES)), _const_spec((1, LANES))],
        out_specs=[wide, wide, wide, wide, _const_spec((m, N_HEADS))],
        out_shape=[jax.ShapeDtypeStruct((m, 1024), BF16),
                   jax.ShapeDtypeStruct((m, 1024), F32),
                   jax.ShapeDtypeStruct((m, 1024), F32),
                   jax.ShapeDtypeStruct((m, 1024), F32),
                   jax.ShapeDtypeStruct((m, N_HEADS), F32)],
        compiler_params=_params("arbitrary"),
        name="fox_proj_rows",
    )(x2d, w_main, w_f, b_f)


def _fox_attn_kernel(qt_ref, k_ref, caug_ref, vt_ref, o_ref,
                     w_ref, sa_ref, sb_ref, mxa_ref, mxb_ref, m_ref, acc_ref, *, t, heads):
    group = pl.program_id(1)
    qi = pl.program_id(2)
    row = lax.broadcasted_iota(jnp.int32, (LANES, t), 0)
    z64 = jnp.zeros((HEAD_DIM, t), BF16)
    for h in range(heads):
        head = heads * group + h
        sel = (row == head) | (row == head + N_HEADS) | (row == head + 2 * N_HEADS)
        sel = jnp.where(sel, 1.0, 0.0).astype(BF16)
        q_h = qt_ref[0, h * HEAD_DIM:(h + 1) * HEAD_DIM, :]
        w_ref[h] = jnp.concatenate(([q_h, z64] if h % 2 == 0 else [z64, q_h]) + [sel], axis=0)
    m_ref[...] = jnp.full_like(m_ref, MASK_VALUE)
    acc_ref[...] = jnp.zeros_like(acc_ref)

    def scores(kt, dst, masked):
        s_ref, mx_ref = dst
        ks = pl.multiple_of(kt * t, t)
        c_tile = caug_ref[0, pl.ds(ks, t), :]
        for h in range(heads):
            k_pair = k_ref[0, pl.ds(ks, t), (h // 2) * LANES:(h // 2 + 1) * LANES]
            s = _dot(jnp.concatenate([k_pair, c_tile], axis=1), w_ref[h])
            if masked:
                kidx = lax.broadcasted_iota(jnp.int32, (t, t), 0)
                qidx = lax.broadcasted_iota(jnp.int32, (t, t), 1)
                s = jnp.where(kidx <= qidx, s, MASK_VALUE)
            s_ref[h] = s
            mx_ref[h] = jnp.max(s, axis=0, keepdims=True)

    ones = jnp.ones((ONES_ROWS, t), BF16)

    def update(kt, src):
        s_ref, mx_ref = src
        ks = pl.multiple_of(kt * t, t)
        for h in range(heads):
            vt = vt_ref[0, h * HEAD_DIM:(h + 1) * HEAD_DIM, pl.ds(ks, t)].astype(BF16)
            m_prev = m_ref[h]
            m_new = jnp.maximum(m_prev, mx_ref[h])
            alpha = jnp.exp2(m_prev - m_new)
            p = jnp.exp2(s_ref[h] - m_new).astype(BF16)
            acc_ref[h] = alpha * acc_ref[h] + _dot(jnp.concatenate([vt, ones], axis=0), p)
            m_ref[h] = m_new

    even_buf = (sa_ref, mxa_ref)
    odd_buf = (sb_ref, mxb_ref)

    def step(kt, masked_next):
        even = kt % 2 == 0

        @pl.when(even)
        def _():
            scores(kt + 1, odd_buf, masked_next)
            update(kt, even_buf)

        @pl.when(jnp.logical_not(even))
        def _():
            scores(kt + 1, even_buf, masked_next)
            update(kt, odd_buf)

    @pl.when(qi == 0)
    def _():
        scores(0, even_buf, True)

    @pl.when(qi > 0)
    def _():
        scores(0, even_buf, False)

        def body(kt, carry):
            step(kt, False)
            return carry

        lax.fori_loop(0, qi - 1, body, 0)
        step(qi - 1, True)

    @pl.when(qi % 2 == 0)
    def _():
        update(qi, even_buf)

    @pl.when(qi % 2 == 1)
    def _():
        update(qi, odd_buf)

    o = jnp.concatenate([acc_ref[h, 0:HEAD_DIM] / acc_ref[h, HEAD_DIM:HEAD_DIM + 1] for h in range(heads)],
                        axis=0)
    o_ref[0] = o.T


def _fox_attn(qt, kbf, caug, vt, t, heads):
    n_seq, _, seq = qt.shape
    width = heads * HEAD_DIM
    return pl.pallas_call(
        functools.partial(_fox_attn_kernel, t=t, heads=heads),
        grid=(n_seq, N_HEADS // heads, seq // t),
        in_specs=[pl.BlockSpec((1, width, t), lambda b, g, i: (b, g, i)),
                  pl.BlockSpec((1, seq, width), lambda b, g, i: (b, 0, g)),
                  pl.BlockSpec((1, seq, LANES), lambda b, g, i: (b, 0, 0)),
                  pl.BlockSpec((1, width, seq), lambda b, g, i: (b, g, 0))],
        out_specs=pl.BlockSpec((1, t, width), lambda b, g, i: (b, i, g)),
        out_shape=jax.ShapeDtypeStruct((n_seq, seq, 1024), F32),
        scratch_shapes=[pltpu.VMEM((heads, 2 * LANES, t), BF16),
                        pltpu.VMEM((heads, t, t), F32),
                        pltpu.VMEM((heads, t, t), F32),
                        pltpu.VMEM((heads, 1, t), F32),
                        pltpu.VMEM((heads, 1, t), F32),
                        pltpu.VMEM((heads, 1, t), F32),
                        pltpu.VMEM((heads, HEAD_DIM + ONES_ROWS, t), F32)],
        compiler_params=_params("arbitrary", "arbitrary", "arbitrary"),
        name="fox_attn",
    )(qt, kbf, caug, vt)


def _fox_decode_kernel(pt_ref, q_ref, kn_ref, vn_ref, lfn_ref, tri_ref, *rest, n_new, group):
    del pt_ref
    k_refs, v_refs, lf_refs = rest[:group], rest[group:2 * group], rest[2 * group:3 * group]
    o_ref, qbd_ref, kbf_ref, vbf_ref, m_ref, l_ref, acc_ref, r_ref = rest[3 * group:]
    step = pl.program_id(1)
    n_rows = n_new * N_HEADS
    row = lax.broadcasted_iota(jnp.int32, (n_rows, 1024), 0)
    lane = lax.broadcasted_iota(jnp.int32, (n_rows, 1024), 1)
    own_head = (lane >> 6) == (row & (N_HEADS - 1))

    def lane_sums(lf, tri):
        out = _dot(jnp.concatenate(_split3(lf), axis=0), tri)
        return out[0:N_HEADS] + out[N_HEADS:2 * N_HEADS] + out[2 * N_HEADS:]

    def widen(x, reps):
        return jnp.concatenate([x] * reps, axis=1)

    @pl.when(step == 0)
    def _():
        q = q_ref[0].astype(F32)
        qb = jnp.concatenate([jnp.broadcast_to(q[t:t + 1], (N_HEADS, 1024)) for t in range(n_new)], axis=0)
        qb = jnp.where(own_head, qb, 0.0)
        qbd_ref[...] = qb.astype(BF16)
        rr = lax.broadcasted_iota(jnp.int32, (LANES, LANES), 0)
        cc = lax.broadcasted_iota(jnp.int32, (LANES, LANES), 1)
        incl = jnp.where(rr <= cc, 1.0, 0.0).astype(BF16)
        bias = -LOG2E * lane_sums(lfn_ref[0], incl)
        bias = jnp.concatenate([bias] * n_new, axis=0)
        j_idx = lax.broadcasted_iota(jnp.int32, (n_rows, LANES), 1)
        t_idx = lax.broadcasted_iota(jnp.int32, (n_rows, LANES), 0) >> 4
        kn = kn_ref[0]
        vn = vn_ref[0]
        s = jnp.full((n_rows, LANES), MASK_VALUE, F32)
        for j in range(n_new):
            sj = jnp.sum(qb * kn[j:j + 1, :], axis=-1, keepdims=True)
            s = jnp.where(j_idx == j, sj, s)
        s = jnp.where(j_idx <= t_idx, s + bias, MASK_VALUE)
        m = jnp.max(s, axis=-1, keepdims=True)
        p = jnp.exp2(s - m)
        m_ref[...] = jnp.broadcast_to(m, (n_rows, LANES))
        l_ref[...] = jnp.broadcast_to(jnp.sum(p, axis=-1, keepdims=True), (n_rows, LANES))
        acc = jnp.zeros((n_rows, 1024), F32)
        for j in range(n_new):
            acc = acc + p[:, j:j + 1] * vn[j:j + 1, :]
        acc_ref[...] = acc
        r_ref[...] = jnp.zeros_like(r_ref)

    r = r_ref[...]
    biases = []
    for i in range(group):
        kbf_ref[:, i * LANES:(i + 1) * LANES] = k_refs[i][0, 0].astype(BF16)
        vbf_ref[:, i * LANES:(i + 1) * LANES] = v_refs[i][0, 0].astype(BF16)
        lf = lf_refs[i][0, 0]
        biases.append(lane_sums(lf, tri_ref[...]) + r)
        r = r + jnp.sum(lf, axis=-1, keepdims=True)
    r_ref[...] = r
    bias = LOG2E * jnp.concatenate(biases, axis=1)
    s = _dot(qbd_ref[...], kbf_ref[...]) + jnp.concatenate([bias] * n_new, axis=0)
    m_prev = m_ref[...]
    m_new = jnp.maximum(m_prev, jnp.max(s, axis=-1, keepdims=True))
    alpha = jnp.exp2(m_prev - m_new)
    p = jnp.exp2(s - widen(m_new, group))
    l_ref[...] = alpha * l_ref[...] + jnp.sum(p, axis=-1, keepdims=True)
    acc_ref[...] = widen(alpha, 1024 // LANES) * acc_ref[...] + _dot_nt(p.astype(BF16), vbf_ref[...])
    m_ref[...] = m_new

    @pl.when(step == pl.num_programs(1) - 1)
    def _():
        o = jnp.where(own_head, acc_ref[...] / widen(l_ref[...], 1024 // LANES), 0.0)
        for t in range(n_new):
            o_ref[0, t:t + 1, :] = jnp.sum(o[t * N_HEADS:(t + 1) * N_HEADS], axis=0, keepdims=True)


def _fox_decode(page_table, q, kn, vn, lfn, cache_kt, cache_vt, cache_lft, layer, group):
    n_b, n_pages = page_table.shape
    n_new = q.shape[1]
    rr = lax.broadcasted_iota(jnp.int32, (LANES, LANES), 0)
    cc = lax.broadcasted_iota(jnp.int32, (LANES, LANES), 1)
    strict = jnp.where(rr > cc, 1.0, 0.0).astype(BF16)

    def page_spec(shape, i):
        return pl.BlockSpec((1, 1) + shape,
                            lambda b, s, pt: (layer, pt[b, n_pages - 1 - (s * group + i)], 0, 0))

    per_b = lambda shape: pl.BlockSpec((1,) + shape, lambda b, s, pt: (b,) + (0,) * len(shape))
    n_rows = n_new * N_HEADS
    grid_spec = pltpu.PrefetchScalarGridSpec(
        num_scalar_prefetch=1,
        grid=(n_b, n_pages // group),
        in_specs=([per_b((n_new, 1024)), per_b((n_new, 1024)), per_b((n_new, 1024)), per_b((N_HEADS, LANES)),
                   pl.BlockSpec((LANES, LANES), lambda b, s, pt: (0, 0))]
                  + [page_spec((1024, LANES), i) for i in range(group)]
                  + [page_spec((1024, LANES), i) for i in range(group)]
                  + [page_spec((N_HEADS, LANES), i) for i in range(group)]),
        out_specs=per_b((n_new, 1024)),
        scratch_shapes=[pltpu.VMEM((n_rows, 1024), BF16),
                        pltpu.VMEM((1024, LANES * group), BF16),
                        pltpu.VMEM((1024, LANES * group), BF16),
                        pltpu.VMEM((n_rows, LANES), F32),
                        pltpu.VMEM((n_rows, LANES), F32),
                        pltpu.VMEM((n_rows, 1024), F32),
                        pltpu.VMEM((N_HEADS, LANES), F32)])
    return pl.pallas_call(
        functools.partial(_fox_decode_kernel, n_new=n_new, group=group),
        grid_spec=grid_spec,
        out_shape=jax.ShapeDtypeStruct((n_b, n_new, 1024), F32),
        compiler_params=_params("arbitrary", "arbitrary"),
        name="fox_decode",
    )(page_table, q, kn, vn, lfn, strict,
      *([cache_kt] * group), *([cache_vt] * group), *([cache_lft] * group))


def _fox_out_kernel(o_ref, g_ref, x_ref, w_ref, lng_ref, lnb_ref, y_ref):
    a = (o_ref[...] * _silu(g_ref[...])).astype(BF16)
    z = DN_ALPHA * x_ref[...] + _dot(a, w_ref[...])
    y_ref[...] = _layer_norm(z, lng_ref[...], lnb_ref[...])


def _fox_out(o2d, g2d, x2d, w_out, ln_g, ln_b, tm):
    m = x2d.shape[0]
    wide = pl.BlockSpec((tm, 1024), lambda i: (i, 0))
    return pl.pallas_call(
        _fox_out_kernel,
        grid=(m // tm,),
        in_specs=[wide, wide, wide, _const_spec((1024, 1024)), _const_spec((1, 1024)),
                  _const_spec((1, 1024))],
        out_specs=wide,
        out_shape=jax.ShapeDtypeStruct((m, 1024), F32),
        compiler_params=_params("arbitrary"),
        name="fox_out",
    )(o2d, g2d, x2d, w_out, ln_g, ln_b)


def _pool_tail(m_bf, gate, x, wg_ref, bg_ref, sc_ref, wo_ref, lng_ref, lnb_ref):
    z = jnp.concatenate([_dot(m_bf[:, g * GROUP_B:(g + 1) * GROUP_B], wg_ref[g])
                         for g in range(len(POOL_WINDOWS))], axis=1) + bg_ref[...]
    a = (z * sc_ref[...] * _silu(gate)).astype(BF16)
    zz = DN_ALPHA * x + _dot(a, wo_ref[...])
    return _layer_norm(zz, lng_ref[...], lnb_ref[...])


def _pool_prompt_kernel(x_ref, wi_ref, wg_ref, bg_ref, sc_ref, wo_ref, lng_ref, lnb_ref,
                        y_ref, hist_ref, ext_ref, m_ref, *, tm):
    i = pl.program_id(1)

    @pl.when(i == 0)
    def _():
        ext_ref[0:HIST_PAD, :] = jnp.zeros((HIST_PAD, WIDTH_B), F32)

    x = x_ref[...]
    xb = x.astype(BF16)
    a = _dot(xb, wi_ref[:, 0:WIDTH_B])
    ext_ref[HIST_PAD:, :] = a
    for g, w in enumerate(POOL_WINDOWS):
        cols = slice(g * GROUP_B, (g + 1) * GROUP_B)
        win = a[:, cols]
        for d in range(1, w):
            win = win + ext_ref[HIST_PAD - d:HIST_PAD - d + tm, cols]
        m_ref[:, cols] = (win * (1.0 / w) - a[:, cols]).astype(BF16)

    @pl.when(i == 0)
    def _():
        pos = lax.broadcasted_iota(jnp.int32, (HIST_PAD, GROUP_B), 0)
        for g, w in enumerate(POOL_WINDOWS):
            cols = slice(g * GROUP_B, (g + 1) * GROUP_B)
            win = ext_ref[HIST_PAD:2 * HIST_PAD, cols]
            for d in range(1, w):
                win = win + ext_ref[HIST_PAD - d:2 * HIST_PAD - d, cols]
            cnt = jnp.minimum(pos + 1, w).astype(F32)
            m_ref[0:HIST_PAD, cols] = (win / cnt - ext_ref[HIST_PAD:2 * HIST_PAD, cols]).astype(BF16)

    gate = _dot(xb, wi_ref[:, WIDTH_B:])
    y_ref[...] = _pool_tail(m_ref[...], gate, x, wg_ref, bg_ref, sc_ref, wo_ref, lng_ref, lnb_ref)
    hist_ref[0] = a[tm - HIST_PAD:, :]
    ext_ref[0:HIST_PAD, :] = a[tm - HIST_PAD:, :]


def _pool_prompt(x2d, w_in, w_grp, b_grp, scale, w_out, ln_g, ln_b, n_seq, tm):
    m = x2d.shape[0]
    nt = m // (n_seq * tm)
    wide = pl.BlockSpec((tm, 1024), lambda b, i: (b * nt + i, 0))
    return pl.pallas_call(
        functools.partial(_pool_prompt_kernel, tm=tm),
        grid=(n_seq, nt),
        in_specs=[wide, _const_spec((1024, 2 * WIDTH_B)), _const_spec((4, GROUP_B, GROUP_B)),
                  _const_spec((1, WIDTH_B)), _const_spec((1, WIDTH_B)), _const_spec((WIDTH_B, 1024)),
                  _const_spec((1, 1024)), _const_spec((1, 1024))],
        out_specs=[wide, pl.BlockSpec((1, HIST_PAD, WIDTH_B), lambda b, i: (b, 0, 0))],
        out_shape=[jax.ShapeDtypeStruct((m, 1024), F32),
                   jax.ShapeDtypeStruct((n_seq, HIST_PAD, WIDTH_B), F32)],
        scratch_shapes=[pltpu.VMEM((HIST_PAD + tm, WIDTH_B), F32),
                        pltpu.VMEM((tm, WIDTH_B), BF16)],
        compiler_params=_params("arbitrary", "arbitrary"),
        name="pool_prompt",
    )(x2d, w_in, w_grp, b_grp, scale, w_out, ln_g, ln_b)


def _pool_sample_kernel(x_ref, hist_ref, wi_ref, wg_ref, bg_ref, sc_ref, wo_ref, lng_ref, lnb_ref,
                        y_ref, newhist_ref, *, n_new):
    xs = [x_ref[t] for t in range(n_new)]
    a = [_dot(x.astype(BF16), wi_ref[:, 0:WIDTH_B]) for x in xs]
    ext = [hist_ref[r] for r in range(POOL_HIST)] + a
    for r in range(POOL_HIST):
        newhist_ref[r] = ext[n_new + r]
    for t in range(n_new):
        parts = []
        for g, w in enumerate(POOL_WINDOWS):
            cols = slice(g * GROUP_B, (g + 1) * GROUP_B)
            win = a[t][:, cols]
            for d in range(1, w):
                win = win + ext[POOL_HIST + t - d][:, cols]
            parts.append(win * (1.0 / w) - a[t][:, cols])
        m_bf = jnp.concatenate(parts, axis=1).astype(BF16)
        gate = _dot(xs[t].astype(BF16), wi_ref[:, WIDTH_B:])
        y_ref[t] = _pool_tail(m_bf, gate, xs[t], wg_ref, bg_ref, sc_ref, wo_ref, lng_ref, lnb_ref)


def _pool_sample(x_tm, hist_tm, w_in, w_grp, b_grp, scale, w_out, ln_g, ln_b):
    n_new, n_b, _ = x_tm.shape
    full = lambda a: _const_spec(a.shape)
    args = (x_tm, hist_tm, w_in, w_grp, b_grp, scale, w_out, ln_g, ln_b)
    return pl.pallas_call(
        functools.partial(_pool_sample_kernel, n_new=n_new),
        grid=(1,),
        in_specs=[full(a) for a in args],
        out_specs=[_const_spec((n_new, n_b, 1024)), _const_spec((POOL_HIST, n_b, WIDTH_B))],
        out_shape=[jax.ShapeDtypeStruct((n_new, n_b, 1024), F32),
                   jax.ShapeDtypeStruct((POOL_HIST, n_b, WIDTH_B), F32)],
        compiler_params=_params("arbitrary"),
        name="pool_sample",
    )(*args)


def _gmlp_kernel(x_ref, wi_ref, vg_ref, vb_ref, ws_ref, bs_ref, wo_ref, lng_ref, lnb_ref,
                 *out_refs, tm, span, emit_v):
    y_ref = out_refs[0]
    x = x_ref[...]
    xb = x.astype(BF16)
    v = _layer_norm(_dot(xb, wi_ref[:, WIDTH_C:2 * WIDTH_C]), vg_ref[...], vb_ref[...])
    if emit_v:
        out_refs[1][...] = v
    vb16 = v.astype(BF16)
    r = lax.broadcasted_iota(jnp.int32, (CHUNK, CHUNK), 0)
    c = lax.broadcasted_iota(jnp.int32, (CHUNK, CHUNK), 1)
    keep = (c <= r) & ((r ^ c) < span)
    chunks = []
    for n in range(tm // CHUNK):
        rows = slice(n * CHUNK, (n + 1) * CHUNK)
        groups = []
        for g in range(N_GROUPS_C):
            wmix = jnp.where(keep, ws_ref[g], 0.0).astype(BF16)
            groups.append(_dot(wmix, vb16[rows, g * GROUP_C:(g + 1) * GROUP_C]))
        chunks.append(jnp.concatenate(groups, axis=1) + bs_ref[...])
    sv = jnp.concatenate(chunks, axis=0) if len(chunks) > 1 else chunks[0]
    u = _dot(xb, wi_ref[:, 0:WIDTH_C])
    gate = _dot(xb, wi_ref[:, 2 * WIDTH_C:])
    a = (u * sv * _silu(gate)).astype(BF16)
    z = DN_ALPHA * x + _dot(a, wo_ref[...])
    y_ref[...] = _layer_norm(z, lng_ref[...], lnb_ref[...])


def _gmlp(x2d, w_in, vg, vb, ws_tiled, bs_rows, w_out, ln_g, ln_b, tm, span, emit_v):
    m = x2d.shape[0]
    wide = pl.BlockSpec((tm, 1024), lambda i: (i, 0))
    out_specs = [wide]
    out_shape = [jax.ShapeDtypeStruct((m, 1024), F32)]
    if emit_v:
        out_specs.append(pl.BlockSpec((tm, WIDTH_C), lambda i: (i, 0)))
        out_shape.append(jax.ShapeDtypeStruct((m, WIDTH_C), F32))
    return pl.pallas_call(
        functools.partial(_gmlp_kernel, tm=tm, span=span, emit_v=emit_v),
        grid=(m // tm,),
        in_specs=[wide, _const_spec((1024, 3 * WIDTH_C)), _const_spec((1, WIDTH_C)),
                  _const_spec((1, WIDTH_C)), _const_spec((N_GROUPS_C, CHUNK, CHUNK)),
                  _const_spec((CHUNK, WIDTH_C)), _const_spec((WIDTH_C, 1024)),
                  _const_spec((1, 1024)), _const_spec((1, 1024))],
        out_specs=out_specs,
        out_shape=out_shape,
        compiler_params=_params("arbitrary"),
        name="gmlp_v" if emit_v else "gmlp",
    )(x2d, w_in, vg, vb, ws_tiled, bs_rows, w_out, ln_g, ln_b)


def _row(v):
    return v.reshape(1, -1)


def _fox_layer(xp, xs, cache_kt, cache_vt, cache_lft, page_table, layer, w_in, b_f, w_out, ln_g, ln_b):
    n_seq, seq, _ = xp.shape
    n_b, n_new, _ = xs.shape
    w_main = w_in[:, :4 * 1024].astype(BF16)
    w_f = jnp.pad(jnp.tile(w_in[:, 4 * 1024:], (1, 3)), ((0, 0), (0, LANES - 3 * N_HEADS))).astype(BF16)
    b_fp = jnp.pad(jnp.tile(b_f, 3), (0, LANES - 3 * N_HEADS)).reshape(1, LANES)
    w_o = w_out.astype(BF16)

    x2d = xp.reshape(n_seq * seq, 1024)
    qt, kt32, vt32, kbf, gate, logf, caug = _fox_proj_prompt(
        x2d, w_main[:, :3 * 1024].T, w_main[:, 1024:2048], w_main[:, 3 * 1024:], w_f, b_fp, n_seq, 512)
    o = _fox_attn(qt, kbf.reshape(n_seq, seq, 1024), caug.reshape(n_seq, seq, LANES), vt32, 512, 4)
    yp = _fox_out(o.reshape(n_seq * seq, 1024), gate, x2d, w_o, ln_g, ln_b, 512)
    out_p = (kt32, vt32, logf.reshape(n_seq, seq, N_HEADS))

    rows = n_b * n_new
    xs2d = xs.reshape(rows, 1024)
    qs, ks32, vs32, gs, lfs = _fox_proj_rows(xs2d, w_main, w_f, b_fp)
    lfn = jnp.pad(lfs.reshape(n_b, n_new, N_HEADS).transpose(0, 2, 1),
                  ((0, 0), (0, 0), (0, LANES - n_new)))
    od = _fox_decode(page_table, qs.reshape(n_b, n_new, 1024), ks32.reshape(n_b, n_new, 1024),
                     vs32.reshape(n_b, n_new, 1024), lfn, cache_kt, cache_vt, cache_lft, layer, 16)
    ys = _fox_out(od.reshape(rows, 1024), gs, xs2d, w_o, ln_g, ln_b, rows)
    out_s = (ks32.reshape(n_b, n_new, N_HEADS, HEAD_DIM), vs32.reshape(n_b, n_new, N_HEADS, HEAD_DIM),
             lfs.reshape(n_b, n_new, N_HEADS))
    return yp.reshape(n_seq, seq, 1024), ys.reshape(n_b, n_new, 1024), out_p, out_s


def _pool_layer(xp, xs, hist, w_in, w_grp, b_grp, scale, w_out, ln_g, ln_b):
    n_seq, seq, _ = xp.shape
    n_b, n_new, _ = xs.shape
    weights = (w_in.astype(BF16), w_grp.astype(BF16), _row(b_grp), _row(scale), w_out.astype(BF16),
               ln_g, ln_b)
    yp, hp = _pool_prompt(xp.reshape(n_seq * seq, 1024), *weights, n_seq, 256)
    ys, hs = _pool_sample(xs.transpose(1, 0, 2), hist.transpose(1, 0, 2), *weights)
    return (yp.reshape(n_seq, seq, 1024), ys.transpose(1, 0, 2),
            hp[:, HIST_PAD - POOL_HIST:, :], hs.transpose(1, 0, 2))


def _gmlp_layer(xp, xs, w_in, vg, vb, w_s, b_s, w_out, ln_g, ln_b):
    n_seq, seq, _ = xp.shape
    n_b, n_new, _ = xs.shape
    rows = n_b * n_new
    shared = (w_in.astype(BF16), _row(vg), _row(vb))
    tail = (w_out.astype(BF16), ln_g, ln_b)
    bs_p = jnp.repeat(b_s.T, GROUP_C, axis=1)
    yp, = _gmlp(xp.reshape(n_seq * seq, 1024), *shared, w_s, bs_p, *tail, 256, CHUNK, False)
    reps = CHUNK // n_new
    ws_s = jnp.tile(w_s[:, :n_new, :n_new], (1, reps, reps))
    bs_s = jnp.tile(jnp.repeat(b_s[:, :n_new].T, GROUP_C, axis=1), (reps, 1))
    ys, vrows = _gmlp(xs.reshape(rows, 1024), *shared, ws_s, bs_s, *tail, rows, n_new, True)
    return yp.reshape(n_seq, seq, 1024), ys.reshape(n_b, n_new, 1024), vrows.reshape(n_b, n_new, WIDTH_C)


def _rows_major(t_l):
    t = jnp.stack(t_l)
    return t.reshape(t.shape[0], t.shape[1], N_HEADS, HEAD_DIM, t.shape[3]).transpose(0, 1, 4, 2, 3)


def kernel(x_prompt, x_sample, cache_k, cache_v, cache_logf, state_pool, page_table, ln_g, ln_b, w_in_a, b_f_a, w_out_a, w_in_b, w_grp_b, b_grp_b, scale_b, w_out_b, w_in_c, ln_v_g, ln_v_b, w_s_c, b_s_c, w_out_c):
    n_layers_a, n_pool, page_size = cache_k.shape[:3]
    ck = cache_k.transpose(0, 1, 3, 4, 2).reshape(n_layers_a, n_pool, N_HEADS * HEAD_DIM, page_size)
    cv = cache_v.transpose(0, 1, 3, 4, 2).reshape(n_layers_a, n_pool, N_HEADS * HEAD_DIM, page_size)
    clft = jnp.swapaxes(cache_logf, 2, 3)
    xp, xs = x_prompt, x_sample
    kp_l, vp_l, fp_l, ks_l, vs_l, fs_l = [], [], [], [], [], []
    poolp_l, pools_l, chunkv_l = [], [], []
    for i in range(DEPTH):
        j = i // N_MIXERS
        kind = i % N_MIXERS
        g, b = _row(ln_g[i]), _row(ln_b[i])
        if kind == 0:
            xp, xs, out_p, out_s = _fox_layer(xp, xs, ck, cv, clft, page_table, j,
                                              w_in_a[j], b_f_a[j], w_out_a[j], g, b)
            kp_l.append(out_p[0]); vp_l.append(out_p[1]); fp_l.append(out_p[2])
            ks_l.append(out_s[0]); vs_l.append(out_s[1]); fs_l.append(out_s[2])
        elif kind == 1:
            xp, xs, hp, hs = _pool_layer(xp, xs, state_pool[j], w_in_b[j], w_grp_b[j], b_grp_b[j],
                                         scale_b[j], w_out_b[j], g, b)
            poolp_l.append(hp); pools_l.append(hs)
        else:
            xp, xs, vrows = _gmlp_layer(xp, xs, w_in_c[j], ln_v_g[j], ln_v_b[j], w_s_c[j], b_s_c[j],
                                        w_out_c[j], g, b)
            chunkv_l.append(vrows)
    return (xp, xs,
            _rows_major(kp_l), _rows_major(vp_l), jnp.stack(fp_l),
            jnp.stack(ks_l), jnp.stack(vs_l), jnp.stack(fs_l),
            jnp.stack(poolp_l), jnp.stack(pools_l),
            jnp.stack(chunkv_l))
```

```python
import functools

import jax
import jax.numpy as jnp
from jax import lax
from jax.experimental import pallas as pl
from jax.experimental.pallas import tpu as pltpu

D_MODEL = 1024
DEPTH = 4
N_MIXERS = 3
N_HEADS = 16
HEAD_DIM = 64
ATTN_SCALE = HEAD_DIM ** -0.5
WIDTH_B = 2048
POOL_WINDOWS = (2, 4, 8, 16)
GROUP_B = 512
POOL_HIST = 15
WIDTH_C = 2048
CHUNK = 128
N_GROUPS_C = 4
GROUP_C = 512
DN_ALPHA = (2 * DEPTH) ** 0.25
LN_EPS = 1e-5
LOG2E = 1.4426950408889634

LANES = 128
HIST_PAD = 16
ONES_ROWS = 16
MASK_VALUE = -1e30
VMEM_LIMIT_BYTES = 56 * 1024 * 1024

F32 = jnp.float32
BF16 = jnp.bfloat16


def _params(*sem):
    return pltpu.CompilerParams(dimension_semantics=sem, vmem_limit_bytes=VMEM_LIMIT_BYTES)


def _dot(a, b):
    return jnp.dot(a, b, preferred_element_type=F32)


def _dot_nt(a, b):
    return lax.dot_general(a, b, (((1,), (1,)), ((), ())), preferred_element_type=F32)


def _silu(x):
    return x / (1.0 + jnp.exp(-x))


def _log_sigmoid(f):
    return jnp.minimum(f, 0.0) - jnp.log1p(jnp.exp(-jnp.abs(f)))


def _layer_norm(z, g, b):
    mu = jnp.mean(z, axis=-1, keepdims=True)
    d = z - mu
    var = jnp.mean(d * d, axis=-1, keepdims=True)
    return d * lax.rsqrt(var + LN_EPS) * g + b


def _split3(x):
    hi = x.astype(BF16)
    r1 = x - hi.astype(F32)
    mid = r1.astype(BF16)
    lo = (r1 - mid.astype(F32)).astype(BF16)
    return hi, mid, lo


def _const_spec(shape):
    return pl.BlockSpec(shape, lambda *_: (0,) * len(shape))


def _fox_proj_prompt_kernel(x_ref, wt_ref, wk_ref, wg_ref, wf_ref, bf_ref, tri_ref,
                            qt_ref, kt_ref, vt_ref, kbf_ref, g_ref, logf_ref, caug_ref, carry_ref):
    @pl.when(pl.program_id(1) == 0)
    def _():
        carry_ref[...] = jnp.zeros_like(carry_ref)

    xb = x_ref[...].astype(BF16)
    qt_ref[0] = (_dot_nt(wt_ref[0:1024, :], xb) * (ATTN_SCALE * LOG2E)).astype(BF16)
    kt_ref[0] = _dot_nt(wt_ref[1024:2048, :], xb)
    vt_ref[0] = _dot_nt(wt_ref[2048:3072, :], xb)
    kbf_ref[...] = _dot(xb, wk_ref[...]).astype(BF16)
    g_ref[...] = _dot(xb, wg_ref[...])

    lf = _log_sigmoid(_dot(xb, wf_ref[...]) + bf_ref[...])
    logf_ref[...] = lf[:, :N_HEADS]
    hi, mid, lo = _split3(lf)
    tri = tri_ref[...]
    c = _dot(tri, hi) + _dot(tri, mid) + _dot(tri, lo) + carry_ref[...]
    carry_ref[...] = c[c.shape[0] - 1:, :]
    neg = -LOG2E * c
    nhi = neg.astype(BF16).astype(F32)
    rem = neg - nhi
    nmid = rem.astype(BF16).astype(F32)
    lane = lax.broadcasted_iota(jnp.int32, c.shape, 1)
    terms = jnp.where(lane < N_HEADS, nhi,
                      jnp.where(lane < 2 * N_HEADS, nmid,
                                jnp.where(lane < 3 * N_HEADS, rem - nmid, 0.0)))
    caug_ref[...] = terms.astype(BF16)


def _fox_proj_prompt(x2d, w_t, w_k, w_g, w_f, b_f, n_seq, tm):
    m = x2d.shape[0]
    seq = m // n_seq
    nt = seq // tm
    tri = jnp.tril(jnp.ones((tm, tm), BF16))
    row = lambda b, i: (b * nt + i, 0)
    wide = pl.BlockSpec((tm, 1024), row)
    tall = pl.BlockSpec((1, 1024, tm), lambda b, i: (b, 0, i))
    return pl.pallas_call(
        _fox_proj_prompt_kernel,
        grid=(n_seq, nt),
        in_specs=[wide, _const_spec((3072, 1024)), _const_spec((1024, 1024)), _const_spec((1024, 1024)),
                  _const_spec((1024, LANES)), _const_spec((1, LANES)), _const_spec((tm, tm))],
        out_specs=[tall, tall, tall, wide, wide, pl.BlockSpec((tm, N_HEADS), row),
                   pl.BlockSpec((tm, LANES), row)],
        out_shape=[jax.ShapeDtypeStruct((n_seq, 1024, seq), BF16),
                   jax.ShapeDtypeStruct((n_seq, 1024, seq), F32),
                   jax.ShapeDtypeStruct((n_seq, 1024, seq), F32),
                   jax.ShapeDtypeStruct((m, 1024), BF16),
                   jax.ShapeDtypeStruct((m, 1024), F32),
                   jax.ShapeDtypeStruct((m, N_HEADS), F32),
                   jax.ShapeDtypeStruct((m, LANES), BF16)],
        scratch_shapes=[pltpu.VMEM((1, LANES), F32)],
        compiler_params=_params("arbitrary", "arbitrary"),
        name="fox_proj_prompt",
    )(x2d, w_t, w_k, w_g, w_f, b_f, tri)


def _fox_proj_rows_kernel(x_ref, w_ref, wf_ref, bf_ref, q_ref, k_ref, v_ref, g_ref, logf_ref):
    xb = x_ref[...].astype(BF16)
    q_ref[...] = (_dot(xb, w_ref[:, 0:1024]) * (ATTN_SCALE * LOG2E)).astype(BF16)
    k_ref[...] = _dot(xb, w_ref[:, 1024:2048])
    v_ref[...] = _dot(xb, w_ref[:, 2048:3072])
    g_ref[...] = _dot(xb, w_ref[:, 3072:4096])
    lf = _log_sigmoid(_dot(xb, wf_ref[...]) + bf_ref[...])
    logf_ref[...] = lf[:, :N_HEADS]


def _fox_proj_rows(x2d, w_main, w_f, b_f):
    m = x2d.shape[0]
    wide = _const_spec((m, 1024))
    return pl.pallas_call(
        _fox_proj_rows_kernel,
        grid=(1,),
        in_specs=[wide, _const_spec((1024, 4096)), _const_spec((1024, LANES)), _const_spec((1, LANES))],
        out_specs=[wide, wide, wide, wide, _const_spec((m, N_HEADS))],
        out_shape=[jax.ShapeDtypeStruct((m, 1024), BF16),
                   jax.ShapeDtypeStruct((m, 1024), F32),
                   jax.ShapeDtypeStruct((m, 1024), F32),
                   jax.ShapeDtypeStruct((m, 1024), F32),
                   jax.ShapeDtypeStruct((m, N_HEADS), F32)],
        compiler_params=_params("arbitrary"),
        name="fox_proj_rows",
    )(x2d, w_main, w_f, b_f)


def _attn_step(group, qi, qt_ref, k_ref, caug_ref, vt_ref, o_ref,
               w_ref, sa_ref, sb_ref, mxa_ref, mxb_ref, m_ref, acc_ref, *, t, heads):
    row = lax.broadcasted_iota(jnp.int32, (LANES, t), 0)
    z64 = jnp.zeros((HEAD_DIM, t), BF16)
    for h in range(heads):
        head = heads * group + h
        sel = (row == head) | (row == head + N_HEADS) | (row == head + 2 * N_HEADS)
        sel = jnp.where(sel, 1.0, 0.0).astype(BF16)
        q_h = qt_ref[0, h * HEAD_DIM:(h + 1) * HEAD_DIM, :]
        w_ref[h] = jnp.concatenate(([q_h, z64] if h % 2 == 0 else [z64, q_h]) + [sel], axis=0)
    m_ref[...] = jnp.full_like(m_ref, MASK_VALUE)
    acc_ref[...] = jnp.zeros_like(acc_ref)

    def scores(kt, dst, masked):
        s_ref, mx_ref = dst
        ks = pl.multiple_of(kt * t, t)
        c_tile = caug_ref[0, pl.ds(ks, t), :]
        for h in range(heads):
            k_pair = k_ref[0, pl.ds(ks, t), (h // 2) * LANES:(h // 2 + 1) * LANES]
            s = _dot(jnp.concatenate([k_pair, c_tile], axis=1), w_ref[h])
            if masked:
                kidx = lax.broadcasted_iota(jnp.int32, (t, t), 0)
                qidx = lax.broadcasted_iota(jnp.int32, (t, t), 1)
                s = jnp.where(kidx <= qidx, s, MASK_VALUE)
            s_ref[h] = s
            mx_ref[h] = jnp.max(s, axis=0, keepdims=True)

    ones = jnp.ones((ONES_ROWS, t), BF16)

    def update(kt, src):
        s_ref, mx_ref = src
        ks = pl.multiple_of(kt * t, t)
        for h in range(heads):
            vt = vt_ref[0, h * HEAD_DIM:(h + 1) * HEAD_DIM, pl.ds(ks, t)].astype(BF16)
            m_prev = m_ref[h]
            m_new = jnp.maximum(m_prev, mx_ref[h])
            alpha = jnp.exp2(m_prev - m_new)
            p = jnp.exp2(s_ref[h] - m_new).astype(BF16)
            acc_ref[h] = alpha * acc_ref[h] + _dot(jnp.concatenate([vt, ones], axis=0), p)
            m_ref[h] = m_new

    even_buf = (sa_ref, mxa_ref)
    odd_buf = (sb_ref, mxb_ref)

    def step(kt, masked_next):
        even = kt % 2 == 0

        @pl.when(even)
        def _():
            scores(kt + 1, odd_buf, masked_next)
            update(kt, even_buf)

        @pl.when(jnp.logical_not(even))
        def _():
            scores(kt + 1, even_buf, masked_next)
            update(kt, odd_buf)

    @pl.when(qi == 0)
    def _():
        scores(0, even_buf, True)

    @pl.when(qi > 0)
    def _():
        scores(0, even_buf, False)

        def body(kt, carry):
            step(kt, False)
            return carry

        lax.fori_loop(0, qi - 1, body, 0)
        step(qi - 1, True)

    @pl.when(qi % 2 == 0)
    def _():
        update(qi, even_buf)

    @pl.when(qi % 2 == 1)
    def _():
        update(qi, odd_buf)

    o = jnp.concatenate([acc_ref[h, 0:HEAD_DIM] / acc_ref[h, HEAD_DIM:HEAD_DIM + 1] for h in range(heads)],
                        axis=0)
    o_ref[0] = o.T


def _decode_step(step, n_steps, q_ref, kn_ref, vn_ref, lfn_ref, tri_ref, k_refs, v_refs, lf_refs,
                 o_ref, qbd_ref, kbf_ref, vbf_ref, m_ref, l_ref, acc_ref, r_ref, *, n_new):
    group = len(k_refs)
    n_rows = n_new * N_HEADS
    row = lax.broadcasted_iota(jnp.int32, (n_rows, 1024), 0)
    lane = lax.broadcasted_iota(jnp.int32, (n_rows, 1024), 1)
    own_head = (lane >> 6) == (row & (N_HEADS - 1))

    def lane_sums(lf, tri):
        out = _dot(jnp.concatenate(_split3(lf), axis=0), tri)
        return out[0:N_HEADS] + out[N_HEADS:2 * N_HEADS] + out[2 * N_HEADS:]

    def widen(x, reps):
        return jnp.concatenate([x] * reps, axis=1)

    @pl.when(step == 0)
    def _():
        q = q_ref[0].astype(F32)
        qb = jnp.concatenate([jnp.broadcast_to(q[t:t + 1], (N_HEADS, 1024)) for t in range(n_new)], axis=0)
        qb = jnp.where(own_head, qb, 0.0)
        qbd_ref[...] = qb.astype(BF16)
        rr = lax.broadcasted_iota(jnp.int32, (LANES, LANES), 0)
        cc = lax.broadcasted_iota(jnp.int32, (LANES, LANES), 1)
        incl = jnp.where(rr <= cc, 1.0, 0.0).astype(BF16)
        bias = -LOG2E * lane_sums(lfn_ref[0], incl)
        bias = jnp.concatenate([bias] * n_new, axis=0)
        j_idx = lax.broadcasted_iota(jnp.int32, (n_rows, LANES), 1)
        t_idx = lax.broadcasted_iota(jnp.int32, (n_rows, LANES), 0) >> 4
        kn = kn_ref[0]
        vn = vn_ref[0]
        s = jnp.full((n_rows, LANES), MASK_VALUE, F32)
        for j in range(n_new):
            sj = jnp.sum(qb * kn[j:j + 1, :], axis=-1, keepdims=True)
            s = jnp.where(j_idx == j, sj, s)
        s = jnp.where(j_idx <= t_idx, s + bias, MASK_VALUE)
        m = jnp.max(s, axis=-1, keepdims=True)
        p = jnp.exp2(s - m)
        m_ref[...] = jnp.broadcast_to(m, (n_rows, LANES))
        l_ref[...] = jnp.broadcast_to(jnp.sum(p, axis=-1, keepdims=True), (n_rows, LANES))
        acc = jnp.zeros((n_rows, 1024), F32)
        for j in range(n_new):
            acc = acc + p[:, j:j + 1] * vn[j:j + 1, :]
        acc_ref[...] = acc
        r_ref[...] = jnp.zeros_like(r_ref)

    r = r_ref[...]
    biases = []
    for i in range(group):
        kbf_ref[:, i * LANES:(i + 1) * LANES] = k_refs[i][0, 0].astype(BF16)
        vbf_ref[:, i * LANES:(i + 1) * LANES] = v_refs[i][0, 0].astype(BF16)
        lf = lf_refs[i][0, 0]
        biases.append(lane_sums(lf, tri_ref[...]) + r)
        r = r + jnp.sum(lf, axis=-1, keepdims=True)
    r_ref[...] = r
    bias = LOG2E * jnp.concatenate(biases, axis=1)
    s = _dot(qbd_ref[...], kbf_ref[...]) + jnp.concatenate([bias] * n_new, axis=0)
    m_prev = m_ref[...]
    m_new = jnp.maximum(m_prev, jnp.max(s, axis=-1, keepdims=True))
    alpha = jnp.exp2(m_prev - m_new)
    p = jnp.exp2(s - widen(m_new, group))
    l_ref[...] = alpha * l_ref[...] + jnp.sum(p, axis=-1, keepdims=True)
    acc_ref[...] = widen(alpha, 1024 // LANES) * acc_ref[...] + _dot_nt(p.astype(BF16), vbf_ref[...])
    m_ref[...] = m_new

    @pl.when(step == n_steps - 1)
    def _():
        o = jnp.where(own_head, acc_ref[...] / widen(l_ref[...], 1024 // LANES), 0.0)
        for t in range(n_new):
            o_ref[0, t:t + 1, :] = jnp.sum(o[t * N_HEADS:(t + 1) * N_HEADS], axis=0, keepdims=True)


N_ATTN_IN, N_DECODE_IN, N_ATTN_SCRATCH = 4, 5, 7


def _fox_attn_decode_kernel(pt_ref, *refs, t, heads, n_new, pages, decode_steps):
    del pt_ref
    attn_in, refs = refs[:N_ATTN_IN], refs[N_ATTN_IN:]
    dec_in, refs = refs[:N_DECODE_IN], refs[N_DECODE_IN:]
    k_refs, v_refs, lf_refs, refs = refs[:pages], refs[pages:2 * pages], refs[2 * pages:3 * pages], refs[3 * pages:]
    o_ref, od_ref, refs = refs[0], refs[1], refs[2:]
    attn_scratch, dec_scratch = refs[:N_ATTN_SCRATCH], refs[N_ATTN_SCRATCH:]
    group = pl.program_id(1)
    qi = pl.program_id(2)
    n = (pl.program_id(0) * pl.num_programs(1) + group) * pl.num_programs(2) + qi
    _decode_step(n % decode_steps, decode_steps, *dec_in, k_refs, v_refs, lf_refs, od_ref, *dec_scratch,
                 n_new=n_new)
    _attn_step(group, qi, *attn_in, o_ref, *attn_scratch, t=t, heads=heads)


def _fox_attn_decode(qt, kbf, caug, vt, page_table, q, kn, vn, lfn, cache_kt, cache_vt, cache_lft, layer,
                     t, heads):
    n_seq, _, seq = qt.shape
    n_b, n_pages = page_table.shape
    n_new = q.shape[1]
    n_groups, n_q = N_HEADS // heads, seq // t
    n_steps = n_seq * n_groups * n_q
    decode_steps = n_steps // n_b
    pages = n_pages // decode_steps
    assert decode_steps * n_b == n_steps and pages * decode_steps == n_pages
    width = heads * HEAD_DIM
    rr = lax.broadcasted_iota(jnp.int32, (LANES, LANES), 0)
    cc = lax.broadcasted_iota(jnp.int32, (LANES, LANES), 1)
    strict = jnp.where(rr > cc, 1.0, 0.0).astype(BF16)

    def step_no(b, g, i):
        return (b * n_groups + g) * n_q + i

    def page_spec(shape, j):
        def index(b, g, i, pt):
            n = step_no(b, g, i)
            return (layer, pt[n // decode_steps, n_pages - 1 - ((n % decode_steps) * pages + j)], 0, 0)
        return pl.BlockSpec((1, 1) + shape, index)

    def per_seq(shape):
        return pl.BlockSpec((1,) + shape,
                            lambda b, g, i, pt: (step_no(b, g, i) // decode_steps,) + (0,) * len(shape))

    n_rows = n_new * N_HEADS
    grid_spec = pltpu.PrefetchScalarGridSpec(
        num_scalar_prefetch=1,
        grid=(n_seq, n_groups, n_q),
        in_specs=([pl.BlockSpec((1, width, t), lambda b, g, i, pt: (b, g, i)),
                   pl.BlockSpec((1, seq, width), lambda b, g, i, pt: (b, 0, g)),
                   pl.BlockSpec((1, seq, LANES), lambda b, g, i, pt: (b, 0, 0)),
                   pl.BlockSpec((1, width, seq), lambda b, g, i, pt: (b, g, 0)),
                   per_seq((n_new, 1024)), per_seq((n_new, 1024)), per_seq((n_new, 1024)),
                   per_seq((N_HEADS, LANES)),
                   pl.BlockSpec((LANES, LANES), lambda b, g, i, pt: (0, 0))]
                  + [page_spec((1024, LANES), j) for j in range(pages)]
                  + [page_spec((1024, LANES), j) for j in range(pages)]
                  + [page_spec((N_HEADS, LANES), j) for j in range(pages)]),
        out_specs=[pl.BlockSpec((1, t, width), lambda b, g, i, pt: (b, i, g)),
                   per_seq((n_new, 1024))],
        scratch_shapes=[pltpu.VMEM((heads, 2 * LANES, t), BF16),
                        pltpu.VMEM((heads, t, t), F32),
                        pltpu.VMEM((heads, t, t), F32),
                        pltpu.VMEM((heads, 1, t), F32),
                        pltpu.VMEM((heads, 1, t), F32),
                        pltpu.VMEM((heads, 1, t), F32),
                        pltpu.VMEM((heads, HEAD_DIM + ONES_ROWS, t), F32),
                        pltpu.VMEM((n_rows, 1024), BF16),
                        pltpu.VMEM((1024, LANES * pages), BF16),
                        pltpu.VMEM((1024, LANES * pages), BF16),
                        pltpu.VMEM((n_rows, LANES), F32),
                        pltpu.VMEM((n_rows, LANES), F32),
                        pltpu.VMEM((n_rows, 1024), F32),
                        pltpu.VMEM((N_HEADS, LANES), F32)])
    return pl.pallas_call(
        functools.partial(_fox_attn_decode_kernel, t=t, heads=heads, n_new=n_new, pages=pages,
                          decode_steps=decode_steps),
        grid_spec=grid_spec,
        out_shape=[jax.ShapeDtypeStruct((n_seq, seq, 1024), F32),
                   jax.ShapeDtypeStruct((n_b, n_new, 1024), F32)],
        compiler_params=_params("arbitrary", "arbitrary", "arbitrary"),
        name="fox_attn_decode",
    )(page_table, qt, kbf, caug, vt, q, kn, vn, lfn, strict,
      *([cache_kt] * pages), *([cache_vt] * pages), *([cache_lft] * pages))


def _fox_out_kernel(o_ref, g_ref, x_ref, w_ref, lng_ref, lnb_ref, y_ref):
    a = (o_ref[...] * _silu(g_ref[...])).astype(BF16)
    z = DN_ALPHA * x_ref[...] + _dot(a, w_ref[...])
    y_ref[...] = _layer_norm(z, lng_ref[...], lnb_ref[...])


def _fox_out(o2d, g2d, x2d, w_out, ln_g, ln_b, tm):
    m = x2d.shape[0]
    wide = pl.BlockSpec((tm, 1024), lambda i: (i, 0))
    return pl.pallas_call(
        _fox_out_kernel,
        grid=(m // tm,),
        in_specs=[wide, wide, wide, _const_spec((1024, 1024)), _const_spec((1, 1024)),
                  _const_spec((1, 1024))],
        out_specs=wide,
        out_shape=jax.ShapeDtypeStruct((m, 1024), F32),
        compiler_params=_params("arbitrary"),
        name="fox_out",
    )(o2d, g2d, x2d, w_out, ln_g, ln_b)


def _pool_tail(m_bf, gate, x, wg_ref, bg_ref, sc_ref, wo_ref, lng_ref, lnb_ref):
    z = jnp.concatenate([_dot(m_bf[:, g * GROUP_B:(g + 1) * GROUP_B], wg_ref[g])
                         for g in range(len(POOL_WINDOWS))], axis=1) + bg_ref[...]
    a = (z * sc_ref[...] * _silu(gate)).astype(BF16)
    zz = DN_ALPHA * x + _dot(a, wo_ref[...])
    return _layer_norm(zz, lng_ref[...], lnb_ref[...])


def _pool_prompt_kernel(x_ref, wi_ref, wg_ref, bg_ref, sc_ref, wo_ref, lng_ref, lnb_ref,
                        y_ref, hist_ref, ext_ref, m_ref, *, tm):
    i = pl.program_id(1)

    @pl.when(i == 0)
    def _():
        ext_ref[0:HIST_PAD, :] = jnp.zeros((HIST_PAD, WIDTH_B), F32)

    x = x_ref[...]
    xb = x.astype(BF16)
    a = _dot(xb, wi_ref[:, 0:WIDTH_B])
    ext_ref[HIST_PAD:, :] = a
    for g, w in enumerate(POOL_WINDOWS):
        cols = slice(g * GROUP_B, (g + 1) * GROUP_B)
        win = a[:, cols]
        for d in range(1, w):
            win = win + ext_ref[HIST_PAD - d:HIST_PAD - d + tm, cols]
        m_ref[:, cols] = (win * (1.0 / w) - a[:, cols]).astype(BF16)

    @pl.when(i == 0)
    def _():
        pos = lax.broadcasted_iota(jnp.int32, (HIST_PAD, GROUP_B), 0)
        for g, w in enumerate(POOL_WINDOWS):
            cols = slice(g * GROUP_B, (g + 1) * GROUP_B)
            win = ext_ref[HIST_PAD:2 * HIST_PAD, cols]
            for d in range(1, w):
                win = win + ext_ref[HIST_PAD - d:2 * HIST_PAD - d, cols]
            cnt = jnp.minimum(pos + 1, w).astype(F32)
            m_ref[0:HIST_PAD, cols] = (win / cnt - ext_ref[HIST_PAD:2 * HIST_PAD, cols]).astype(BF16)

    gate = _dot(xb, wi_ref[:, WIDTH_B:])
    y_ref[...] = _pool_tail(m_ref[...], gate, x, wg_ref, bg_ref, sc_ref, wo_ref, lng_ref, lnb_ref)
    hist_ref[0] = a[tm - HIST_PAD:, :]
    ext_ref[0:HIST_PAD, :] = a[tm - HIST_PAD:, :]


def _pool_prompt(x2d, w_in, w_grp, b_grp, scale, w_out, ln_g, ln_b, n_seq, tm):
    m = x2d.shape[0]
    nt = m // (n_seq * tm)
    wide = pl.BlockSpec((tm, 1024), lambda b, i: (b * nt + i, 0))
    return pl.pallas_call(
        functools.partial(_pool_prompt_kernel, tm=tm),
        grid=(n_seq, nt),
        in_specs=[wide, _const_spec((1024, 2 * WIDTH_B)), _const_spec((4, GROUP_B, GROUP_B)),
                  _const_spec((1, WIDTH_B)), _const_spec((1, WIDTH_B)), _const_spec((WIDTH_B, 1024)),
                  _const_spec((1, 1024)), _const_spec((1, 1024))],
        out_specs=[wide, pl.BlockSpec((1, HIST_PAD, WIDTH_B), lambda b, i: (b, 0, 0))],
        out_shape=[jax.ShapeDtypeStruct((m, 1024), F32),
                   jax.ShapeDtypeStruct((n_seq, HIST_PAD, WIDTH_B), F32)],
        scratch_shapes=[pltpu.VMEM((HIST_PAD + tm, WIDTH_B), F32),
                        pltpu.VMEM((tm, WIDTH_B), BF16)],
        compiler_params=_params("arbitrary", "arbitrary"),
        name="pool_prompt",
    )(x2d, w_in, w_grp, b_grp, scale, w_out, ln_g, ln_b)


def _pool_sample_kernel(x_ref, hist_ref, wi_ref, wg_ref, bg_ref, sc_ref, wo_ref, lng_ref, lnb_ref,
                        y_ref, newhist_ref, *, n_new):
    xs = [x_ref[t] for t in range(n_new)]
    a = [_dot(x.astype(BF16), wi_ref[:, 0:WIDTH_B]) for x in xs]
    ext = [hist_ref[r] for r in range(POOL_HIST)] + a
    for r in range(POOL_HIST):
        newhist_ref[r] = ext[n_new + r]
    for t in range(n_new):
        parts = []
        for g, w in enumerate(POOL_WINDOWS):
            cols = slice(g * GROUP_B, (g + 1) * GROUP_B)
            win = a[t][:, cols]
            for d in range(1, w):
                win = win + ext[POOL_HIST + t - d][:, cols]
            parts.append(win * (1.0 / w) - a[t][:, cols])
        m_bf = jnp.concatenate(parts, axis=1).astype(BF16)
        gate = _dot(xs[t].astype(BF16), wi_ref[:, WIDTH_B:])
        y_ref[t] = _pool_tail(m_bf, gate, xs[t], wg_ref, bg_ref, sc_ref, wo_ref, lng_ref, lnb_ref)


def _pool_sample(x_tm, hist_tm, w_in, w_grp, b_grp, scale, w_out, ln_g, ln_b):
    n_new, n_b, _ = x_tm.shape
    full = lambda a: _const_spec(a.shape)
    args = (x_tm, hist_tm, w_in, w_grp, b_grp, scale, w_out, ln_g, ln_b)
    return pl.pallas_call(
        functools.partial(_pool_sample_kernel, n_new=n_new),
        grid=(1,),
        in_specs=[full(a) for a in args],
        out_specs=[_const_spec((n_new, n_b, 1024)), _const_spec((POOL_HIST, n_b, WIDTH_B))],
        out_shape=[jax.ShapeDtypeStruct((n_new, n_b, 1024), F32),
                   jax.ShapeDtypeStruct((POOL_HIST, n_b, WIDTH_B), F32)],
        compiler_params=_params("arbitrary"),
        name="pool_sample",
    )(*args)


def _gmlp_kernel(x_ref, wi_ref, vg_ref, vb_ref, ws_ref, bs_ref, wo_ref, lng_ref, lnb_ref,
                 *out_refs, tm, span, emit_v):
    y_ref = out_refs[0]
    x = x_ref[...]
    xb = x.astype(BF16)
    v = _layer_norm(_dot(xb, wi_ref[:, WIDTH_C:2 * WIDTH_C]), vg_ref[...], vb_ref[...])
    if emit_v:
        out_refs[1][...] = v
    vb16 = v.astype(BF16)
    r = lax.broadcasted_iota(jnp.int32, (CHUNK, CHUNK), 0)
    c = lax.broadcasted_iota(jnp.int32, (CHUNK, CHUNK), 1)
    keep = (c <= r) & ((r ^ c) < span)
    chunks = []
    for n in range(tm // CHUNK):
        rows = slice(n * CHUNK, (n + 1) * CHUNK)
        groups = []
        for g in range(N_GROUPS_C):
            wmix = jnp.where(keep, ws_ref[g], 0.0).astype(BF16)
            groups.append(_dot(wmix, vb16[rows, g * GROUP_C:(g + 1) * GROUP_C]))
        chunks.append(jnp.concatenate(groups, axis=1) + bs_ref[...])
    sv = jnp.concatenate(chunks, axis=0) if len(chunks) > 1 else chunks[0]
    u = _dot(xb, wi_ref[:, 0:WIDTH_C])
    gate = _dot(xb, wi_ref[:, 2 * WIDTH_C:])
    a = (u * sv * _silu(gate)).astype(BF16)
    z = DN_ALPHA * x + _dot(a, wo_ref[...])
    y_ref[...] = _layer_norm(z, lng_ref[...], lnb_ref[...])


def _gmlp(x2d, w_in, vg, vb, ws_tiled, bs_rows, w_out, ln_g, ln_b, tm, span, emit_v):
    m = x2d.shape[0]
    wide = pl.BlockSpec((tm, 1024), lambda i: (i, 0))
    out_specs = [wide]
    out_shape = [jax.ShapeDtypeStruct((m, 1024), F32)]
    if emit_v:
        out_specs.append(pl.BlockSpec((tm, WIDTH_C), lambda i: (i, 0)))
        out_shape.append(jax.ShapeDtypeStruct((m, WIDTH_C), F32))
    return pl.pallas_call(
        functools.partial(_gmlp_kernel, tm=tm, span=span, emit_v=emit_v),
        grid=(m // tm,),
        in_specs=[wide, _const_spec((1024, 3 * WIDTH_C)), _const_spec((1, WIDTH_C)),
                  _const_spec((1, WIDTH_C)), _const_spec((N_GROUPS_C, CHUNK, CHUNK)),
                  _const_spec((CHUNK, WIDTH_C)), _const_spec((WIDTH_C, 1024)),
                  _const_spec((1, 1024)), _const_spec((1, 1024))],
        out_specs=out_specs,
        out_shape=out_shape,
        compiler_params=_params("arbitrary"),
        name="gmlp_v" if emit_v else "gmlp",
    )(x2d, w_in, vg, vb, ws_tiled, bs_rows, w_out, ln_g, ln_b)


def _row(v):
    return v.reshape(1, -1)


def _fox_layer(xp, xs, cache_kt, cache_vt, cache_lft, page_table, layer, w_in, b_f, w_out, ln_g, ln_b):
    n_seq, seq, _ = xp.shape
    n_b, n_new, _ = xs.shape
    w_main = w_in[:, :4 * 1024].astype(BF16)
    w_f = jnp.pad(jnp.tile(w_in[:, 4 * 1024:], (1, 3)), ((0, 0), (0, LANES - 3 * N_HEADS))).astype(BF16)
    b_fp = jnp.pad(jnp.tile(b_f, 3), (0, LANES - 3 * N_HEADS)).reshape(1, LANES)
    w_o = w_out.astype(BF16)

    x2d = xp.reshape(n_seq * seq, 1024)
    qt, kt32, vt32, kbf, gate, logf, caug = _fox_proj_prompt(
        x2d, w_main[:, :3 * 1024].T, w_main[:, 1024:2048], w_main[:, 3 * 1024:], w_f, b_fp, n_seq, 512)
    out_p = (kt32, vt32, logf.reshape(n_seq, seq, N_HEADS))

    rows = n_b * n_new
    xs2d = xs.reshape(rows, 1024)
    qs, ks32, vs32, gs, lfs = _fox_proj_rows(xs2d, w_main, w_f, b_fp)
    lfn = jnp.pad(lfs.reshape(n_b, n_new, N_HEADS).transpose(0, 2, 1),
                  ((0, 0), (0, 0), (0, LANES - n_new)))

    o, od = _fox_attn_decode(qt, kbf.reshape(n_seq, seq, 1024), caug.reshape(n_seq, seq, LANES), vt32,
                             page_table, qs.reshape(n_b, n_new, 1024), ks32.reshape(n_b, n_new, 1024),
                             vs32.reshape(n_b, n_new, 1024), lfn, cache_kt, cache_vt, cache_lft, layer,
                             512, 2)
    yp = _fox_out(o.reshape(n_seq * seq, 1024), gate, x2d, w_o, ln_g, ln_b, 512)
    ys = _fox_out(od.reshape(rows, 1024), gs, xs2d, w_o, ln_g, ln_b, rows)
    out_s = (ks32.reshape(n_b, n_new, N_HEADS, HEAD_DIM), vs32.reshape(n_b, n_new, N_HEADS, HEAD_DIM),
             lfs.reshape(n_b, n_new, N_HEADS))
    return yp.reshape(n_seq, seq, 1024), ys.reshape(n_b, n_new, 1024), out_p, out_s


def _pool_layer(xp, xs, hist, w_in, w_grp, b_grp, scale, w_out, ln_g, ln_b):
    n_seq, seq, _ = xp.shape
    n_b, n_new, _ = xs.shape
    weights = (w_in.astype(BF16), w_grp.astype(BF16), _row(b_grp), _row(scale), w_out.astype(BF16),
               ln_g, ln_b)
    yp, hp = _pool_prompt(xp.reshape(n_seq * seq, 1024), *weights, n_seq, 256)
    ys, hs = _pool_sample(xs.transpose(1, 0, 2), hist.transpose(1, 0, 2), *weights)
    return (yp.reshape(n_seq, seq, 1024), ys.transpose(1, 0, 2),
            hp[:, HIST_PAD - POOL_HIST:, :], hs.transpose(1, 0, 2))


def _gmlp_layer(xp, xs, w_in, vg, vb, w_s, b_s, w_out, ln_g, ln_b):
    n_seq, seq, _ = xp.shape
    n_b, n_new, _ = xs.shape
    rows = n_b * n_new
    shared = (w_in.astype(BF16), _row(vg), _row(vb))
    tail = (w_out.astype(BF16), ln_g, ln_b)
    bs_p = jnp.repeat(b_s.T, GROUP_C, axis=1)
    yp, = _gmlp(xp.reshape(n_seq * seq, 1024), *shared, w_s, bs_p, *tail, 256, CHUNK, False)
    reps = CHUNK // n_new
    ws_s = jnp.tile(w_s[:, :n_new, :n_new], (1, reps, reps))
    bs_s = jnp.tile(jnp.repeat(b_s[:, :n_new].T, GROUP_C, axis=1), (reps, 1))
    ys, vrows = _gmlp(xs.reshape(rows, 1024), *shared, ws_s, bs_s, *tail, rows, n_new, True)
    return yp.reshape(n_seq, seq, 1024), ys.reshape(n_b, n_new, 1024), vrows.reshape(n_b, n_new, WIDTH_C)


def _rows_major(t_l):
    t = jnp.stack(t_l)
    return t.reshape(t.shape[0], t.shape[1], N_HEADS, HEAD_DIM, t.shape[3]).transpose(0, 1, 4, 2, 3)


def kernel(x_prompt, x_sample, cache_k, cache_v, cache_logf, state_pool, page_table, ln_g, ln_b, w_in_a, b_f_a, w_out_a, w_in_b, w_grp_b, b_grp_b, scale_b, w_out_b, w_in_c, ln_v_g, ln_v_b, w_s_c, b_s_c, w_out_c):
    n_layers_a, n_pool, page_size = cache_k.shape[:3]
    ck = cache_k.transpose(0, 1, 3, 4, 2).reshape(n_layers_a, n_pool, N_HEADS * HEAD_DIM, page_size)
    cv = cache_v.transpose(0, 1, 3, 4, 2).reshape(n_layers_a, n_pool, N_HEADS * HEAD_DIM, page_size)
    clft = jnp.swapaxes(cache_logf, 2, 3)
    xp, xs = x_prompt, x_sample
    kp_l, vp_l, fp_l, ks_l, vs_l, fs_l = [], [], [], [], [], []
    poolp_l, pools_l, chunkv_l = [], [], []
    for i in range(DEPTH):
        j = i // N_MIXERS
        kind = i % N_MIXERS
        g, b = _row(ln_g[i]), _row(ln_b[i])
        if kind == 0:
            xp, xs, out_p, out_s = _fox_layer(xp, xs, ck, cv, clft, page_table, j,
                                              w_in_a[j], b_f_a[j], w_out_a[j], g, b)
            kp_l.append(out_p[0]); vp_l.append(out_p[1]); fp_l.append(out_p[2])
            ks_l.append(out_s[0]); vs_l.append(out_s[1]); fs_l.append(out_s[2])
        elif kind == 1:
            xp, xs, hp, hs = _pool_layer(xp, xs, state_pool[j], w_in_b[j], w_grp_b[j], b_grp_b[j],
                                         scale_b[j], w_out_b[j], g, b)
            poolp_l.append(hp); pools_l.append(hs)
        else:
            xp, xs, vrows = _gmlp_layer(xp, xs, w_in_c[j], ln_v_g[j], ln_v_b[j], w_s_c[j], b_s_c[j],
                                        w_out_c[j], g, b)
            chunkv_l.append(vrows)
    return (xp, xs,
            _rows_major(kp_l), _rows_major(vp_l), jnp.stack(fp_l),
            jnp.stack(ks_l), jnp.stack(vs_l), jnp.stack(fs_l),
            jnp.stack(poolp_l), jnp.stack(pools_l),
            jnp.stack(chunkv_l))
```

```python
import functools

import jax
import jax.numpy as jnp
from jax import lax
from jax.experimental import pallas as pl
from jax.experimental.pallas import tpu as pltpu

D_MODEL = 1024
DEPTH = 4
N_MIXERS = 3
N_HEADS = 16
HEAD_DIM = 64
ATTN_SCALE = HEAD_DIM ** -0.5
WIDTH_B = 2048
POOL_WINDOWS = (2, 4, 8, 16)
GROUP_B = 512
POOL_HIST = 15
WIDTH_C = 2048
CHUNK = 128
N_GROUPS_C = 4
GROUP_C = 512
DN_ALPHA = (2 * DEPTH) ** 0.25
LN_EPS = 1e-5
LOG2E = 1.4426950408889634

LANES = 128
HIST_PAD = 16
ONES_ROWS = 16
MASK_VALUE = -1e30
VMEM_LIMIT_BYTES = 56 * 1024 * 1024

F32 = jnp.float32
BF16 = jnp.bfloat16


def _params(*sem):
    return pltpu.CompilerParams(dimension_semantics=sem, vmem_limit_bytes=VMEM_LIMIT_BYTES)


def _dot(a, b):
    return jnp.dot(a, b, preferred_element_type=F32)


def _dot_nt(a, b):
    return lax.dot_general(a, b, (((1,), (1,)), ((), ())), preferred_element_type=F32)


def _silu(x):
    return x / (1.0 + jnp.exp(-x))


def _log_sigmoid(f):
    return jnp.minimum(f, 0.0) - jnp.log1p(jnp.exp(-jnp.abs(f)))


def _layer_norm(z, g, b):
    mu = jnp.mean(z, axis=-1, keepdims=True)
    d = z - mu
    var = jnp.mean(d * d, axis=-1, keepdims=True)
    return d * lax.rsqrt(var + LN_EPS) * g + b


def _split3(x):
    hi = x.astype(BF16)
    r1 = x - hi.astype(F32)
    mid = r1.astype(BF16)
    lo = (r1 - mid.astype(F32)).astype(BF16)
    return hi, mid, lo


def _const_spec(shape):
    return pl.BlockSpec(shape, lambda *_: (0,) * len(shape))


def _fox_proj_prompt_kernel(x_ref, wt_ref, wk_ref, wg_ref, wf_ref, bf_ref, tri_ref,
                            qt_ref, kt_ref, vt_ref, kbf_ref, g_ref, logf_ref, caug_ref, carry_ref):
    @pl.when(pl.program_id(1) == 0)
    def _():
        carry_ref[...] = jnp.zeros_like(carry_ref)

    xb = x_ref[...].astype(BF16)
    qt_ref[0] = (_dot_nt(wt_ref[0:1024, :], xb) * (ATTN_SCALE * LOG2E)).astype(BF16)
    kt_ref[0] = _dot_nt(wt_ref[1024:2048, :], xb)
    vt_ref[0] = _dot_nt(wt_ref[2048:3072, :], xb)
    kbf_ref[...] = _dot(xb, wk_ref[...]).astype(BF16)
    g_ref[...] = _dot(xb, wg_ref[...])

    lf = _log_sigmoid(_dot(xb, wf_ref[...]) + bf_ref[...])
    logf_ref[...] = lf[:, :N_HEADS]
    hi, mid, lo = _split3(lf)
    tri = tri_ref[...]
    c = _dot(tri, hi) + _dot(tri, mid) + _dot(tri, lo) + carry_ref[...]
    carry_ref[...] = c[c.shape[0] - 1:, :]
    neg = -LOG2E * c
    nhi = neg.astype(BF16).astype(F32)
    rem = neg - nhi
    nmid = rem.astype(BF16).astype(F32)
    lane = lax.broadcasted_iota(jnp.int32, c.shape, 1)
    terms = jnp.where(lane < N_HEADS, nhi,
                      jnp.where(lane < 2 * N_HEADS, nmid,
                                jnp.where(lane < 3 * N_HEADS, rem - nmid, 0.0)))
    caug_ref[...] = terms.astype(BF16)


def _fox_proj_prompt(x2d, w_t, w_k, w_g, w_f, b_f, n_seq, tm):
    m = x2d.shape[0]
    seq = m // n_seq
    nt = seq // tm
    tri = jnp.tril(jnp.ones((tm, tm), BF16))
    row = lambda b, i: (b * nt + i, 0)
    wide = pl.BlockSpec((tm, 1024), row)
    tall = pl.BlockSpec((1, 1024, tm), lambda b, i: (b, 0, i))
    return pl.pallas_call(
        _fox_proj_prompt_kernel,
        grid=(n_seq, nt),
        in_specs=[wide, _const_spec((3072, 1024)), _const_spec((1024, 1024)), _const_spec((1024, 1024)),
                  _const_spec((1024, LANES)), _const_spec((1, LANES)), _const_spec((tm, tm))],
        out_specs=[tall, tall, tall, wide, wide, pl.BlockSpec((tm, N_HEADS), row),
                   pl.BlockSpec((tm, LANES), row)],
        out_shape=[jax.ShapeDtypeStruct((n_seq, 1024, seq), BF16),
                   jax.ShapeDtypeStruct((n_seq, 1024, seq), F32),
                   jax.ShapeDtypeStruct((n_seq, 1024, seq), F32),
                   jax.ShapeDtypeStruct((m, 1024), BF16),
                   jax.ShapeDtypeStruct((m, 1024), F32),
                   jax.ShapeDtypeStruct((m, N_HEADS), F32),
                   jax.ShapeDtypeStruct((m, LANES), BF16)],
        scratch_shapes=[pltpu.VMEM((1, LANES), F32)],
        compiler_params=_params("arbitrary", "arbitrary"),
        name="fox_proj_prompt",
    )(x2d, w_t, w_k, w_g, w_f, b_f, tri)


def _fox_proj_rows_kernel(x_ref, w_ref, wf_ref, bf_ref, q_ref, k_ref, v_ref, g_ref, logf_ref):
    xb = x_ref[...].astype(BF16)
    q_ref[...] = (_dot(xb, w_ref[:, 0:1024]) * (ATTN_SCALE * LOG2E)).astype(BF16)
    k_ref[...] = _dot(xb, w_ref[:, 1024:2048])
    v_ref[...] = _dot(xb, w_ref[:, 2048:3072])
    g_ref[...] = _dot(xb, w_ref[:, 3072:4096])
    lf = _log_sigmoid(_dot(xb, wf_ref[...]) + bf_ref[...])
    logf_ref[...] = lf[:, :N_HEADS]


def _fox_proj_rows(x2d, w_main, w_f, b_f):
    m = x2d.shape[0]
    wide = _const_spec((m, 1024))
    return pl.pallas_call(
        _fox_proj_rows_kernel,
        grid=(1,),
        in_specs=[wide, _const_spec((1024, 4096)), _const_spec((1024, LANES)), _const_spec((1, LANES))],
        out_specs=[wide, wide, wide, wide, _const_spec((m, N_HEADS))],
        out_shape=[jax.ShapeDtypeStruct((m, 1024), BF16),
                   jax.ShapeDtypeStruct((m, 1024), F32),
                   jax.ShapeDtypeStruct((m, 1024), F32),
                   jax.ShapeDtypeStruct((m, 1024), F32),
                   jax.ShapeDtypeStruct((m, N_HEADS), F32)],
        compiler_params=_params("arbitrary"),
        name="fox_proj_rows",
    )(x2d, w_main, w_f, b_f)


def _fox_attn_kernel(qt_ref, k_ref, caug_ref, vt_ref, o_ref,
                     w_ref, sa_ref, sb_ref, mxa_ref, mxb_ref, m_ref, acc_ref, *, t, heads, chunks):
    group = pl.program_id(1)
    qi = pl.program_id(2)
    row = lax.broadcasted_iota(jnp.int32, (LANES, t), 0)
    z64 = jnp.zeros((HEAD_DIM, t), BF16)
    for h in range(heads):
        head = heads * group + h
        sel = (row == head) | (row == head + N_HEADS) | (row == head + 2 * N_HEADS)
        sel = jnp.where(sel, 1.0, 0.0).astype(BF16)
        q_h = qt_ref[0, h * HEAD_DIM:(h + 1) * HEAD_DIM, :]
        w_ref[h] = jnp.concatenate(([q_h, z64] if h % 2 == 0 else [z64, q_h]) + [sel], axis=0)
    m_ref[...] = jnp.full_like(m_ref, MASK_VALUE)
    acc_ref[...] = jnp.zeros_like(acc_ref)

    tc = t // chunks
    pieces = tuple((h, c) for h in range(heads) for c in range(chunks))

    def scores(kt, dst, masked, which=pieces):
        s_ref, mx_ref = dst
        ks = pl.multiple_of(kt * t, t)
        c_tile = caug_ref[0, pl.ds(ks, t), :]
        for h, c in which:
            cols = slice(c * tc, (c + 1) * tc)
            k_pair = k_ref[0, pl.ds(ks, t), (h // 2) * LANES:(h // 2 + 1) * LANES]
            s = _dot(jnp.concatenate([k_pair, c_tile], axis=1), w_ref[h, :, cols])
            if masked:
                kidx = lax.broadcasted_iota(jnp.int32, (t, tc), 0)
                qidx = lax.broadcasted_iota(jnp.int32, (t, tc), 1) + c * tc
                s = jnp.where(kidx <= qidx, s, MASK_VALUE)
            s_ref[h, :, cols] = s
            mx_ref[h, :, cols] = jnp.max(s, axis=0, keepdims=True)

    ones = jnp.ones((ONES_ROWS, t), BF16)

    def update(kt, src, which=pieces):
        s_ref, mx_ref = src
        ks = pl.multiple_of(kt * t, t)
        for h, c in which:
            cols = slice(c * tc, (c + 1) * tc)
            vt = vt_ref[0, h * HEAD_DIM:(h + 1) * HEAD_DIM, pl.ds(ks, t)].astype(BF16)
            m_prev = m_ref[h, :, cols]
            m_new = jnp.maximum(m_prev, mx_ref[h, :, cols])
            alpha = jnp.exp2(m_prev - m_new)
            p = jnp.exp2(s_ref[h, :, cols] - m_new).astype(BF16)
            acc_ref[h, :, cols] = alpha * acc_ref[h, :, cols] + _dot(jnp.concatenate([vt, ones], axis=0), p)
            m_ref[h, :, cols] = m_new

    even_buf = (sa_ref, mxa_ref)
    odd_buf = (sb_ref, mxb_ref)

    def step(kt, masked_next):
        even = kt % 2 == 0

        def both(cur, nxt):
            for piece in pieces:
                scores(kt + 1, nxt, masked_next, [piece])
                update(kt, cur, [piece])

        @pl.when(even)
        def _():
            both(even_buf, odd_buf)

        @pl.when(jnp.logical_not(even))
        def _():
            both(odd_buf, even_buf)

    @pl.when(qi == 0)
    def _():
        scores(0, even_buf, True)

    @pl.when(qi > 0)
    def _():
        scores(0, even_buf, False)

        def body(kt, carry):
            step(kt, False)
            return carry

        lax.fori_loop(0, qi - 1, body, 0)
        step(qi - 1, True)

    @pl.when(qi % 2 == 0)
    def _():
        update(qi, even_buf)

    @pl.when(qi % 2 == 1)
    def _():
        update(qi, odd_buf)

    o = jnp.concatenate([acc_ref[h, 0:HEAD_DIM] / acc_ref[h, HEAD_DIM:HEAD_DIM + 1] for h in range(heads)],
                        axis=0)
    o_ref[0] = o.T


def _fox_attn(qt, kbf, caug, vt, t, heads, chunks):
    n_seq, _, seq = qt.shape
    width = heads * HEAD_DIM
    return pl.pallas_call(
        functools.partial(_fox_attn_kernel, t=t, heads=heads, chunks=chunks),
        grid=(n_seq, N_HEADS // heads, seq // t),
        in_specs=[pl.BlockSpec((1, width, t), lambda b, g, i: (b, g, i)),
                  pl.BlockSpec((1, seq, width), lambda b, g, i: (b, 0, g)),
                  pl.BlockSpec((1, seq, LANES), lambda b, g, i: (b, 0, 0)),
                  pl.BlockSpec((1, width, seq), lambda b, g, i: (b, g, 0))],
        out_specs=pl.BlockSpec((1, t, width), lambda b, g, i: (b, i, g)),
        out_shape=jax.ShapeDtypeStruct((n_seq, seq, 1024), F32),
        scratch_shapes=[pltpu.VMEM((heads, 2 * LANES, t), BF16),
                        pltpu.VMEM((heads, t, t), F32),
                        pltpu.VMEM((heads, t, t), F32),
                        pltpu.VMEM((heads, 1, t), F32),
                        pltpu.VMEM((heads, 1, t), F32),
                        pltpu.VMEM((heads, 1, t), F32),
                        pltpu.VMEM((heads, HEAD_DIM + ONES_ROWS, t), F32)],
        compiler_params=_params("arbitrary", "arbitrary", "arbitrary"),
        name="fox_attn",
    )(qt, kbf, caug, vt)


def _fox_decode_kernel(pt_ref, q_ref, kn_ref, vn_ref, lfn_ref, tri_ref, *rest, n_new, group):
    del pt_ref
    k_refs, v_refs, lf_refs = rest[:group], rest[group:2 * group], rest[2 * group:3 * group]
    o_ref, qbd_ref, kbf_ref, vbf_ref, m_ref, l_ref, acc_ref, r_ref = rest[3 * group:]
    step = pl.program_id(1)
    n_rows = n_new * N_HEADS
    row = lax.broadcasted_iota(jnp.int32, (n_rows, 1024), 0)
    lane = lax.broadcasted_iota(jnp.int32, (n_rows, 1024), 1)
    own_head = (lane >> 6) == (row & (N_HEADS - 1))

    def lane_sums(lf, tri):
        out = _dot(jnp.concatenate(_split3(lf), axis=0), tri)
        return out[0:N_HEADS] + out[N_HEADS:2 * N_HEADS] + out[2 * N_HEADS:]

    def widen(x, reps):
        return jnp.concatenate([x] * reps, axis=1)

    @pl.when(step == 0)
    def _():
        q = q_ref[0].astype(F32)
        qb = jnp.concatenate([jnp.broadcast_to(q[t:t + 1], (N_HEADS, 1024)) for t in range(n_new)], axis=0)
        qb = jnp.where(own_head, qb, 0.0)
        qbd_ref[...] = qb.astype(BF16)
        rr = lax.broadcasted_iota(jnp.int32, (LANES, LANES), 0)
        cc = lax.broadcasted_iota(jnp.int32, (LANES, LANES), 1)
        incl = jnp.where(rr <= cc, 1.0, 0.0).astype(BF16)
        bias = -LOG2E * lane_sums(lfn_ref[0], incl)
        bias = jnp.concatenate([bias] * n_new, axis=0)
        j_idx = lax.broadcasted_iota(jnp.int32, (n_rows, LANES), 1)
        t_idx = lax.broadcasted_iota(jnp.int32, (n_rows, LANES), 0) >> 4
        kn = kn_ref[0]
        vn = vn_ref[0]
        s = jnp.full((n_rows, LANES), MASK_VALUE, F32)
        for j in range(n_new):
            sj = jnp.sum(qb * kn[j:j + 1, :], axis=-1, keepdims=True)
            s = jnp.where(j_idx == j, sj, s)
        s = jnp.where(j_idx <= t_idx, s + bias, MASK_VALUE)
        m = jnp.max(s, axis=-1, keepdims=True)
        p = jnp.exp2(s - m)
        m_ref[...] = jnp.broadcast_to(m, (n_rows, LANES))
        l_ref[...] = jnp.broadcast_to(jnp.sum(p, axis=-1, keepdims=True), (n_rows, LANES))
        acc = jnp.zeros((n_rows, 1024), F32)
        for j in range(n_new):
            acc = acc + p[:, j:j + 1] * vn[j:j + 1, :]
        acc_ref[...] = acc
        r_ref[...] = jnp.zeros_like(r_ref)

    r = r_ref[...]
    biases = []
    for i in range(group):
        kbf_ref[:, i * LANES:(i + 1) * LANES] = k_refs[i][0, 0].astype(BF16)
        vbf_ref[:, i * LANES:(i + 1) * LANES] = v_refs[i][0, 0].astype(BF16)
        lf = lf_refs[i][0, 0]
        biases.append(lane_sums(lf, tri_ref[...]) + r)
        r = r + jnp.sum(lf, axis=-1, keepdims=True)
    r_ref[...] = r
    bias = LOG2E * jnp.concatenate(biases, axis=1)
    s = _dot(qbd_ref[...], kbf_ref[...]) + jnp.concatenate([bias] * n_new, axis=0)
    m_prev = m_ref[...]
    m_new = jnp.maximum(m_prev, jnp.max(s, axis=-1, keepdims=True))
    alpha = jnp.exp2(m_prev - m_new)
    p = jnp.exp2(s - widen(m_new, group))
    l_ref[...] = alpha * l_ref[...] + jnp.sum(p, axis=-1, keepdims=True)
    acc_ref[...] = widen(alpha, 1024 // LANES) * acc_ref[...] + _dot_nt(p.astype(BF16), vbf_ref[...])
    m_ref[...] = m_new

    @pl.when(step == pl.num_programs(1) - 1)
    def _():
        o = jnp.where(own_head, acc_ref[...] / widen(l_ref[...], 1024 // LANES), 0.0)
        for t in range(n_new):
            o_ref[0, t:t + 1, :] = jnp.sum(o[t * N_HEADS:(t + 1) * N_HEADS], axis=0, keepdims=True)


def _fox_decode(page_table, q, kn, vn, lfn, cache_kt, cache_vt, cache_lft, layer, group):
    n_b, n_pages = page_table.shape
    n_new = q.shape[1]
    rr = lax.broadcasted_iota(jnp.int32, (LANES, LANES), 0)
    cc = lax.broadcasted_iota(jnp.int32, (LANES, LANES), 1)
    strict = jnp.where(rr > cc, 1.0, 0.0).astype(BF16)

    def page_spec(shape, i):
        return pl.BlockSpec((1, 1) + shape,
                            lambda b, s, pt: (layer, pt[b, n_pages - 1 - (s * group + i)], 0, 0))

    per_b = lambda shape: pl.BlockSpec((1,) + shape, lambda b, s, pt: (b,) + (0,) * len(shape))
    n_rows = n_new * N_HEADS
    grid_spec = pltpu.PrefetchScalarGridSpec(
        num_scalar_prefetch=1,
        grid=(n_b, n_pages // group),
        in_specs=([per_b((n_new, 1024)), per_b((n_new, 1024)), per_b((n_new, 1024)), per_b((N_HEADS, LANES)),
                   pl.BlockSpec((LANES, LANES), lambda b, s, pt: (0, 0))]
                  + [page_spec((1024, LANES), i) for i in range(group)]
                  + [page_spec((1024, LANES), i) for i in range(group)]
                  + [page_spec((N_HEADS, LANES), i) for i in range(group)]),
        out_specs=per_b((n_new, 1024)),
        scratch_shapes=[pltpu.VMEM((n_rows, 1024), BF16),
                        pltpu.VMEM((1024, LANES * group), BF16),
                        pltpu.VMEM((1024, LANES * group), BF16),
                        pltpu.VMEM((n_rows, LANES), F32),
                        pltpu.VMEM((n_rows, LANES), F32),
                        pltpu.VMEM((n_rows, 1024), F32),
                        pltpu.VMEM((N_HEADS, LANES), F32)])
    return pl.pallas_call(
        functools.partial(_fox_decode_kernel, n_new=n_new, group=group),
        grid_spec=grid_spec,
        out_shape=jax.ShapeDtypeStruct((n_b, n_new, 1024), F32),
        compiler_params=_params("arbitrary", "arbitrary"),
        name="fox_decode",
    )(page_table, q, kn, vn, lfn, strict,
      *([cache_kt] * group), *([cache_vt] * group), *([cache_lft] * group))


def _fox_out_kernel(o_ref, g_ref, x_ref, w_ref, lng_ref, lnb_ref, y_ref):
    a = (o_ref[...] * _silu(g_ref[...])).astype(BF16)
    z = DN_ALPHA * x_ref[...] + _dot(a, w_ref[...])
    y_ref[...] = _layer_norm(z, lng_ref[...], lnb_ref[...])


def _fox_out(o2d, g2d, x2d, w_out, ln_g, ln_b, tm):
    m = x2d.shape[0]
    wide = pl.BlockSpec((tm, 1024), lambda i: (i, 0))
    return pl.pallas_call(
        _fox_out_kernel,
        grid=(m // tm,),
        in_specs=[wide, wide, wide, _const_spec((1024, 1024)), _const_spec((1, 1024)),
                  _const_spec((1, 1024))],
        out_specs=wide,
        out_shape=jax.ShapeDtypeStruct((m, 1024), F32),
        compiler_params=_params("arbitrary"),
        name="fox_out",
    )(o2d, g2d, x2d, w_out, ln_g, ln_b)


def _pool_tail(m_bf, gate, x, wg_ref, bg_ref, sc_ref, wo_ref, lng_ref, lnb_ref):
    z = jnp.concatenate([_dot(m_bf[:, g * GROUP_B:(g + 1) * GROUP_B], wg_ref[g])
                         for g in range(len(POOL_WINDOWS))], axis=1) + bg_ref[...]
    a = (z * sc_ref[...] * _silu(gate)).astype(BF16)
    zz = DN_ALPHA * x + _dot(a, wo_ref[...])
    return _layer_norm(zz, lng_ref[...], lnb_ref[...])


def _pool_prompt_kernel(x_ref, wi_ref, wg_ref, bg_ref, sc_ref, wo_ref, lng_ref, lnb_ref,
                        y_ref, hist_ref, ext_ref, *, tm):
    i = pl.program_id(1)

    @pl.when(i == 0)
    def _():
        ext_ref[0:HIST_PAD, :] = jnp.zeros((HIST_PAD, WIDTH_B), F32)

    x = x_ref[...]
    xb = x.astype(BF16)
    pos = i * tm + lax.broadcasted_iota(jnp.int32, (tm, 1), 0)
    y = DN_ALPHA * x
    for g, w in enumerate(POOL_WINDOWS):
        cols = slice(g * GROUP_B, (g + 1) * GROUP_B)
        a = _dot(xb, wi_ref[:, cols])
        ext_ref[HIST_PAD:, cols] = a
        win = a
        for d in range(1, w):
            win = win + ext_ref[HIST_PAD - d:HIST_PAD - d + tm, cols]
        inv_count = 1.0 / jnp.minimum(pos + 1, w).astype(F32)
        m = (win * inv_count - a).astype(BF16)
        gate = _dot(xb, wi_ref[:, WIDTH_B + g * GROUP_B:WIDTH_B + (g + 1) * GROUP_B])
        z = _dot(m, wg_ref[g]) + bg_ref[:, cols]
        act = (z * sc_ref[:, cols] * _silu(gate)).astype(BF16)
        y = y + _dot(act, wo_ref[cols, :])
        hist_ref[0, :, cols] = a[tm - HIST_PAD:, :]
        ext_ref[0:HIST_PAD, cols] = a[tm - HIST_PAD:, :]
    y_ref[...] = _layer_norm(y, lng_ref[...], lnb_ref[...])


def _pool_prompt(x2d, w_in, w_grp, b_grp, scale, w_out, ln_g, ln_b, n_seq, tm):
    m = x2d.shape[0]
    nt = m // (n_seq * tm)
    wide = pl.BlockSpec((tm, 1024), lambda b, i: (b * nt + i, 0))
    return pl.pallas_call(
        functools.partial(_pool_prompt_kernel, tm=tm),
        grid=(n_seq, nt),
        in_specs=[wide, _const_spec((1024, 2 * WIDTH_B)), _const_spec((4, GROUP_B, GROUP_B)),
                  _const_spec((1, WIDTH_B)), _const_spec((1, WIDTH_B)), _const_spec((WIDTH_B, 1024)),
                  _const_spec((1, 1024)), _const_spec((1, 1024))],
        out_specs=[wide, pl.BlockSpec((1, HIST_PAD, WIDTH_B), lambda b, i: (b, 0, 0))],
        out_shape=[jax.ShapeDtypeStruct((m, 1024), F32),
                   jax.ShapeDtypeStruct((n_seq, HIST_PAD, WIDTH_B), F32)],
        scratch_shapes=[pltpu.VMEM((HIST_PAD + tm, WIDTH_B), F32)],
        compiler_params=_params("arbitrary", "arbitrary"),
        name="pool_prompt",
    )(x2d, w_in, w_grp, b_grp, scale, w_out, ln_g, ln_b)


def _pool_sample_kernel(x_ref, hist_ref, wi_ref, wg_ref, bg_ref, sc_ref, wo_ref, lng_ref, lnb_ref,
                        y_ref, newhist_ref, *, n_new):
    xs = [x_ref[t] for t in range(n_new)]
    a = [_dot(x.astype(BF16), wi_ref[:, 0:WIDTH_B]) for x in xs]
    ext = [hist_ref[r] for r in range(POOL_HIST)] + a
    for r in range(POOL_HIST):
        newhist_ref[r] = ext[n_new + r]
    for t in range(n_new):
        parts = []
        for g, w in enumerate(POOL_WINDOWS):
            cols = slice(g * GROUP_B, (g + 1) * GROUP_B)
            win = a[t][:, cols]
            for d in range(1, w):
                win = win + ext[POOL_HIST + t - d][:, cols]
            parts.append(win * (1.0 / w) - a[t][:, cols])
        m_bf = jnp.concatenate(parts, axis=1).astype(BF16)
        gate = _dot(xs[t].astype(BF16), wi_ref[:, WIDTH_B:])
        y_ref[t] = _pool_tail(m_bf, gate, xs[t], wg_ref, bg_ref, sc_ref, wo_ref, lng_ref, lnb_ref)


def _pool_sample(x_tm, hist_tm, w_in, w_grp, b_grp, scale, w_out, ln_g, ln_b):
    n_new, n_b, _ = x_tm.shape
    full = lambda a: _const_spec(a.shape)
    args = (x_tm, hist_tm, w_in, w_grp, b_grp, scale, w_out, ln_g, ln_b)
    return pl.pallas_call(
        functools.partial(_pool_sample_kernel, n_new=n_new),
        grid=(1,),
        in_specs=[full(a) for a in args],
        out_specs=[_const_spec((n_new, n_b, 1024)), _const_spec((POOL_HIST, n_b, WIDTH_B))],
        out_shape=[jax.ShapeDtypeStruct((n_new, n_b, 1024), F32),
                   jax.ShapeDtypeStruct((POOL_HIST, n_b, WIDTH_B), F32)],
        compiler_params=_params("arbitrary"),
        name="pool_sample",
    )(*args)


def _gmlp_kernel(x_ref, wi_ref, vg_ref, vb_ref, ws_ref, bs_ref, wo_ref, lng_ref, lnb_ref,
                 *out_refs, tm, span, emit_v):
    y_ref = out_refs[0]
    x = x_ref[...]
    xb = x.astype(BF16)
    v = _layer_norm(_dot(xb, wi_ref[:, WIDTH_C:2 * WIDTH_C]), vg_ref[...], vb_ref[...])
    if emit_v:
        out_refs[1][...] = v
    vb16 = v.astype(BF16)
    r = lax.broadcasted_iota(jnp.int32, (CHUNK, CHUNK), 0)
    c = lax.broadcasted_iota(jnp.int32, (CHUNK, CHUNK), 1)
    keep = (c <= r) & ((r ^ c) < span)
    chunks = []
    for n in range(tm // CHUNK):
        rows = slice(n * CHUNK, (n + 1) * CHUNK)
        groups = []
        for g in range(N_GROUPS_C):
            wmix = jnp.where(keep, ws_ref[g], 0.0).astype(BF16)
            groups.append(_dot(wmix, vb16[rows, g * GROUP_C:(g + 1) * GROUP_C]))
        chunks.append(jnp.concatenate(groups, axis=1) + bs_ref[...])
    sv = jnp.concatenate(chunks, axis=0) if len(chunks) > 1 else chunks[0]
    u = _dot(xb, wi_ref[:, 0:WIDTH_C])
    gate = _dot(xb, wi_ref[:, 2 * WIDTH_C:])
    a = (u * sv * _silu(gate)).astype(BF16)
    z = DN_ALPHA * x + _dot(a, wo_ref[...])
    y_ref[...] = _layer_norm(z, lng_ref[...], lnb_ref[...])


def _gmlp(x2d, w_in, vg, vb, ws_tiled, bs_rows, w_out, ln_g, ln_b, tm, span, emit_v):
    m = x2d.shape[0]
    wide = pl.BlockSpec((tm, 1024), lambda i: (i, 0))
    out_specs = [wide]
    out_shape = [jax.ShapeDtypeStruct((m, 1024), F32)]
    if emit_v:
        out_specs.append(pl.BlockSpec((tm, WIDTH_C), lambda i: (i, 0)))
        out_shape.append(jax.ShapeDtypeStruct((m, WIDTH_C), F32))
    return pl.pallas_call(
        functools.partial(_gmlp_kernel, tm=tm, span=span, emit_v=emit_v),
        grid=(m // tm,),
        in_specs=[wide, _const_spec((1024, 3 * WIDTH_C)), _const_spec((1, WIDTH_C)),
                  _const_spec((1, WIDTH_C)), _const_spec((N_GROUPS_C, CHUNK, CHUNK)),
                  _const_spec((CHUNK, WIDTH_C)), _const_spec((WIDTH_C, 1024)),
                  _const_spec((1, 1024)), _const_spec((1, 1024))],
        out_specs=out_specs,
        out_shape=out_shape,
        compiler_params=_params("arbitrary"),
        name="gmlp_v" if emit_v else "gmlp",
    )(x2d, w_in, vg, vb, ws_tiled, bs_rows, w_out, ln_g, ln_b)


def _row(v):
    return v.reshape(1, -1)


def _fox_layer(xp, xs, cache_kt, cache_vt, cache_lft, page_table, layer, w_in, b_f, w_out, ln_g, ln_b):
    n_seq, seq, _ = xp.shape
    n_b, n_new, _ = xs.shape
    w_main = w_in[:, :4 * 1024].astype(BF16)
    w_f = jnp.pad(jnp.tile(w_in[:, 4 * 1024:], (1, 3)), ((0, 0), (0, LANES - 3 * N_HEADS))).astype(BF16)
    b_fp = jnp.pad(jnp.tile(b_f, 3), (0, LANES - 3 * N_HEADS)).reshape(1, LANES)
    w_o = w_out.astype(BF16)

    x2d = xp.reshape(n_seq * seq, 1024)
    qt, kt32, vt32, kbf, gate, logf, caug = _fox_proj_prompt(
        x2d, w_main[:, :3 * 1024].T, w_main[:, 1024:2048], w_main[:, 3 * 1024:], w_f, b_fp, n_seq, 512)
    o = _fox_attn(qt, kbf.reshape(n_seq, seq, 1024), caug.reshape(n_seq, seq, LANES), vt32, 512, 4, 1)
    yp = _fox_out(o.reshape(n_seq * seq, 1024), gate, x2d, w_o, ln_g, ln_b, 512)
    out_p = (kt32, vt32, logf.reshape(n_seq, seq, N_HEADS))

    rows = n_b * n_new
    xs2d = xs.reshape(rows, 1024)
    qs, ks32, vs32, gs, lfs = _fox_proj_rows(xs2d, w_main, w_f, b_fp)
    lfn = jnp.pad(lfs.reshape(n_b, n_new, N_HEADS).transpose(0, 2, 1),
                  ((0, 0), (0, 0), (0, LANES - n_new)))
    od = _fox_decode(page_table, qs.reshape(n_b, n_new, 1024), ks32.reshape(n_b, n_new, 1024),
                     vs32.reshape(n_b, n_new, 1024), lfn, cache_kt, cache_vt, cache_lft, layer, 16)
    ys = _fox_out(od.reshape(rows, 1024), gs, xs2d, w_o, ln_g, ln_b, rows)
    out_s = (ks32.reshape(n_b, n_new, N_HEADS, HEAD_DIM), vs32.reshape(n_b, n_new, N_HEADS, HEAD_DIM),
             lfs.reshape(n_b, n_new, N_HEADS))
    return yp.reshape(n_seq, seq, 1024), ys.reshape(n_b, n_new, 1024), out_p, out_s


def _pool_layer(xp, xs, hist, w_in, w_grp, b_grp, scale, w_out, ln_g, ln_b):
    n_seq, seq, _ = xp.shape
    n_b, n_new, _ = xs.shape
    weights = (w_in.astype(BF16), w_grp.astype(BF16), _row(b_grp), _row(scale), w_out.astype(BF16),
               ln_g, ln_b)
    yp, hp = _pool_prompt(xp.reshape(n_seq * seq, 1024), *weights, n_seq, 256)
    ys, hs = _pool_sample(xs.transpose(1, 0, 2), hist.transpose(1, 0, 2), *weights)
    return (yp.reshape(n_seq, seq, 1024), ys.transpose(1, 0, 2),
            hp[:, HIST_PAD - POOL_HIST:, :], hs.transpose(1, 0, 2))


def _gmlp_layer(xp, xs, w_in, vg, vb, w_s, b_s, w_out, ln_g, ln_b):
    n_seq, seq, _ = xp.shape
    n_b, n_new, _ = xs.shape
    rows = n_b * n_new
    shared = (w_in.astype(BF16), _row(vg), _row(vb))
    tail = (w_out.astype(BF16), ln_g, ln_b)
    bs_p = jnp.repeat(b_s.T, GROUP_C, axis=1)
    yp, = _gmlp(xp.reshape(n_seq * seq, 1024), *shared, w_s, bs_p, *tail, 256, CHUNK, False)
    reps = CHUNK // n_new
    ws_s = jnp.tile(w_s[:, :n_new, :n_new], (1, reps, reps))
    bs_s = jnp.tile(jnp.repeat(b_s[:, :n_new].T, GROUP_C, axis=1), (reps, 1))
    ys, vrows = _gmlp(xs.reshape(rows, 1024), *shared, ws_s, bs_s, *tail, rows, n_new, True)
    return yp.reshape(n_seq, seq, 1024), ys.reshape(n_b, n_new, 1024), vrows.reshape(n_b, n_new, WIDTH_C)


def _rows_major(t_l):
    t = jnp.stack(t_l)
    return t.reshape(t.shape[0], t.shape[1], N_HEADS, HEAD_DIM, t.shape[3]).transpose(0, 1, 4, 2, 3)


def kernel(x_prompt, x_sample, cache_k, cache_v, cache_logf, state_pool, page_table, ln_g, ln_b, w_in_a, b_f_a, w_out_a, w_in_b, w_grp_b, b_grp_b, scale_b, w_out_b, w_in_c, ln_v_g, ln_v_b, w_s_c, b_s_c, w_out_c):
    n_layers_a, n_pool, page_size = cache_k.shape[:3]
    ck = cache_k.transpose(0, 1, 3, 4, 2).reshape(n_layers_a, n_pool, N_HEADS * HEAD_DIM, page_size)
    cv = cache_v.transpose(0, 1, 3, 4, 2).reshape(n_layers_a, n_pool, N_HEADS * HEAD_DIM, page_size)
    clft = jnp.swapaxes(cache_logf, 2, 3)
    xp, xs = x_prompt, x_sample
    kp_l, vp_l, fp_l, ks_l, vs_l, fs_l = [], [], [], [], [], []
    poolp_l, pools_l, chunkv_l = [], [], []
    for i in range(DEPTH):
        j = i // N_MIXERS
        kind = i % N_MIXERS
        g, b = _row(ln_g[i]), _row(ln_b[i])
        if kind == 0:
            xp, xs, out_p, out_s = _fox_layer(xp, xs, ck, cv, clft, page_table, j,
                                              w_in_a[j], b_f_a[j], w_out_a[j], g, b)
            kp_l.append(out_p[0]); vp_l.append(out_p[1]); fp_l.append(out_p[2])
            ks_l.append(out_s[0]); vs_l.append(out_s[1]); fs_l.append(out_s[2])
        elif kind == 1:
            xp, xs, hp, hs = _pool_layer(xp, xs, state_pool[j], w_in_b[j], w_grp_b[j], b_grp_b[j],
                                         scale_b[j], w_out_b[j], g, b)
            poolp_l.append(hp); pools_l.append(hs)
        else:
            xp, xs, vrows = _gmlp_layer(xp, xs, w_in_c[j], ln_v_g[j], ln_v_b[j], w_s_c[j], b_s_c[j],
                                        w_out_c[j], g, b)
            chunkv_l.append(vrows)
    return (xp, xs,
            _rows_major(kp_l), _rows_major(vp_l), jnp.stack(fp_l),
            jnp.stack(ks_l), jnp.stack(vs_l), jnp.stack(fs_l),
            jnp.stack(poolp_l), jnp.stack(pools_l),
            jnp.stack(chunkv_l))
```

```python
import functools

import jax
import jax.numpy as jnp
from jax import lax
from jax.experimental import pallas as pl
from jax.experimental.pallas import tpu as pltpu

D_MODEL = 1024
DEPTH = 4
N_MIXERS = 3
N_HEADS = 16
HEAD_DIM = 64
ATTN_SCALE = HEAD_DIM ** -0.5
WIDTH_B = 2048
POOL_WINDOWS = (2, 4, 8, 16)
GROUP_B = 512
POOL_HIST = 15
WIDTH_C = 2048
CHUNK = 128
N_GROUPS_C = 4
GROUP_C = 512
DN_ALPHA = (2 * DEPTH) ** 0.25
LN_EPS = 1e-5
LOG2E = 1.4426950408889634

LANES = 128
HIST_PAD = 16
ONES_ROWS = 16
MASK_VALUE = -1e30
VMEM_LIMIT_BYTES = 56 * 1024 * 1024

F32 = jnp.float32
BF16 = jnp.bfloat16


def _params(*sem):
    return pltpu.CompilerParams(dimension_semantics=sem, vmem_limit_bytes=VMEM_LIMIT_BYTES)


def _dot(a, b):
    return jnp.dot(a, b, preferred_element_type=F32)


def _dot_nt(a, b):
    return lax.dot_general(a, b, (((1,), (1,)), ((), ())), preferred_element_type=F32)


def _silu(x):
    return x / (1.0 + jnp.exp(-x))


def _log_sigmoid(f):
    return jnp.minimum(f, 0.0) - jnp.log1p(jnp.exp(-jnp.abs(f)))


def _layer_norm(z, g, b):
    mu = jnp.mean(z, axis=-1, keepdims=True)
    d = z - mu
    var = jnp.mean(d * d, axis=-1, keepdims=True)
    return d * lax.rsqrt(var + LN_EPS) * g + b


def _split3(x):
    hi = x.astype(BF16)
    r1 = x - hi.astype(F32)
    mid = r1.astype(BF16)
    lo = (r1 - mid.astype(F32)).astype(BF16)
    return hi, mid, lo


def _const_spec(shape):
    return pl.BlockSpec(shape, lambda *_: (0,) * len(shape))


def _fox_proj_prompt_kernel(x_ref, wt_ref, wk_ref, wg_ref, wf_ref, bf_ref, tri_ref, *rest, n_prev):
    if n_prev:
        pk_ref, pv_ref = rest[:2]
        rest = rest[2:]
    qt_ref, kt_ref, vt_ref, kbf_ref, g_ref, logf_ref, caug_ref, carry_ref = rest
    slot = pl.program_id(2)

    if n_prev:
        @pl.when(slot < n_prev)
        def _():
            kt_ref[0, 0] = pk_ref[0, 0]
            vt_ref[0, 0] = pv_ref[0, 0]

    @pl.when(slot == n_prev)
    def _():
        _fox_proj_prompt_tile(x_ref, wt_ref, wk_ref, wg_ref, wf_ref, bf_ref, tri_ref,
                              qt_ref, kt_ref, vt_ref, kbf_ref, g_ref, logf_ref, caug_ref, carry_ref)


def _fox_proj_prompt_tile(x_ref, wt_ref, wk_ref, wg_ref, wf_ref, bf_ref, tri_ref,
                          qt_ref, kt_ref, vt_ref, kbf_ref, g_ref, logf_ref, caug_ref, carry_ref):
    @pl.when(pl.program_id(1) == 0)
    def _():
        carry_ref[...] = jnp.zeros_like(carry_ref)

    xb = x_ref[...].astype(BF16)
    qt_ref[0] = (_dot_nt(wt_ref[0:1024, :], xb) * (ATTN_SCALE * LOG2E)).astype(BF16)
    kt_ref[0, 0] = _dot_nt(wt_ref[1024:2048, :], xb)
    vt_ref[0, 0] = _dot_nt(wt_ref[2048:3072, :], xb)
    kbf_ref[...] = _dot(xb, wk_ref[...]).astype(BF16)
    g_ref[...] = _dot(xb, wg_ref[...])

    lf = _log_sigmoid(_dot(xb, wf_ref[...]) + bf_ref[...])
    logf_ref[...] = lf[:, :N_HEADS]
    hi, mid, lo = _split3(lf)
    tri = tri_ref[...]
    c = _dot(tri, hi) + _dot(tri, mid) + _dot(tri, lo) + carry_ref[...]
    carry_ref[...] = c[c.shape[0] - 1:, :]
    neg = -LOG2E * c
    nhi = neg.astype(BF16).astype(F32)
    rem = neg - nhi
    nmid = rem.astype(BF16).astype(F32)
    lane = lax.broadcasted_iota(jnp.int32, c.shape, 1)
    terms = jnp.where(lane < N_HEADS, nhi,
                      jnp.where(lane < 2 * N_HEADS, nmid,
                                jnp.where(lane < 3 * N_HEADS, rem - nmid, 0.0)))
    caug_ref[...] = terms.astype(BF16)


def _fox_proj_prompt(x2d, w_t, w_k, w_g, w_f, b_f, prev_kt, prev_vt, n_seq, tm):
    m = x2d.shape[0]
    seq = m // n_seq
    nt = seq // tm
    n_prev = 0 if prev_kt is None else prev_kt.shape[0]
    tri = jnp.tril(jnp.ones((tm, tm), BF16))
    const = lambda shape: pl.BlockSpec(shape, lambda b, i, s: (0,) * len(shape))
    row = lambda b, i, s: (b * nt + i, 0)
    wide = pl.BlockSpec((tm, 1024), row)
    tall = pl.BlockSpec((1, 1024, tm), lambda b, i, s: (b, 0, i))
    stacked = pl.BlockSpec((1, 1, 1024, tm), lambda b, i, s: (s, b, 0, i))
    prev = pl.BlockSpec((1, 1, 1024, tm), lambda b, i, s: (jnp.minimum(s, n_prev - 1), b, 0, i))
    stack_shape = jax.ShapeDtypeStruct((n_prev + 1, n_seq, 1024, seq), F32)
    return pl.pallas_call(
        functools.partial(_fox_proj_prompt_kernel, n_prev=n_prev),
        grid=(n_seq, nt, n_prev + 1),
        in_specs=[wide, const((3072, 1024)), const((1024, 1024)), const((1024, 1024)),
                  const((1024, LANES)), const((1, LANES)), const((tm, tm))] + [prev, prev] * bool(n_prev),
        out_specs=[tall, stacked, stacked, wide, wide, pl.BlockSpec((tm, N_HEADS), row),
                   pl.BlockSpec((tm, LANES), row)],
        out_shape=[jax.ShapeDtypeStruct((n_seq, 1024, seq), BF16),
                   stack_shape,
                   stack_shape,
                   jax.ShapeDtypeStruct((m, 1024), BF16),
                   jax.ShapeDtypeStruct((m, 1024), F32),
                   jax.ShapeDtypeStruct((m, N_HEADS), F32),
                   jax.ShapeDtypeStruct((m, LANES), BF16)],
        scratch_shapes=[pltpu.VMEM((1, LANES), F32)],
        compiler_params=_params("arbitrary", "arbitrary", "arbitrary"),
        name="fox_proj_prompt",
    )(x2d, w_t, w_k, w_g, w_f, b_f, tri, *([prev_kt, prev_vt] * bool(n_prev)))


def _fox_proj_rows_kernel(x_ref, w_ref, wf_ref, bf_ref, q_ref, k_ref, v_ref, g_ref, logf_ref):
    xb = x_ref[...].astype(BF16)
    q_ref[...] = (_dot(xb, w_ref[:, 0:1024]) * (ATTN_SCALE * LOG2E)).astype(BF16)
    k_ref[...] = _dot(xb, w_ref[:, 1024:2048])
    v_ref[...] = _dot(xb, w_ref[:, 2048:3072])
    g_ref[...] = _dot(xb, w_ref[:, 3072:4096])
    lf = _log_sigmoid(_dot(xb, wf_ref[...]) + bf_ref[...])
    logf_ref[...] = lf[:, :N_HEADS]


def _fox_proj_rows(x2d, w_main, w_f, b_f):
    m = x2d.shape[0]
    wide = _const_spec((m, 1024))
    return pl.pallas_call(
        _fox_proj_rows_kernel,
        grid=(1,),
        in_specs=[wide, _const_spec((1024, 4096)), _const_spec((1024, LANES)), _const_spec((1, LANES))],
        out_specs=[wide, wide, wide, wide, _const_spec((m, N_HEADS))],
        out_shape=[jax.ShapeDtypeStruct((m, 1024), BF16),
                   jax.ShapeDtypeStruct((m, 1024), F32),
                   jax.ShapeDtypeStruct((m, 1024), F32),
                   jax.ShapeDtypeStruct((m, 1024), F32),
                   jax.ShapeDtypeStruct((m, N_HEADS), F32)],
        compiler_params=_params("arbitrary"),
        name="fox_proj_rows",
    )(x2d, w_main, w_f, b_f)


def _fox_attn_kernel(qt_ref, k_ref, caug_ref, vt_ref, o_ref,
                     w_ref, sa_ref, sb_ref, mxa_ref, mxb_ref, m_ref, acc_ref, *, t, heads, chunks):
    group = pl.program_id(1)
    qi = pl.program_id(2)
    row = lax.broadcasted_iota(jnp.int32, (LANES, t), 0)
    z64 = jnp.zeros((HEAD_DIM, t), BF16)
    for h in range(heads):
        head = heads * group + h
        sel = (row == head) | (row == head + N_HEADS) | (row == head + 2 * N_HEADS)
        sel = jnp.where(sel, 1.0, 0.0).astype(BF16)
        q_h = qt_ref[0, h * HEAD_DIM:(h + 1) * HEAD_DIM, :]
        w_ref[h] = jnp.concatenate(([q_h, z64] if h % 2 == 0 else [z64, q_h]) + [sel], axis=0)
    m_ref[...] = jnp.full_like(m_ref, MASK_VALUE)
    acc_ref[...] = jnp.zeros_like(acc_ref)

    tc = t // chunks
    pieces = tuple((h, c) for h in range(heads) for c in range(chunks))

    def scores(kt, dst, masked, which=pieces):
        s_ref, mx_ref = dst
        ks = pl.multiple_of(kt * t, t)
        c_tile = caug_ref[0, pl.ds(ks, t), :]
        for h, c in which:
            cols = slice(c * tc, (c + 1) * tc)
            k_pair = k_ref[0, pl.ds(ks, t), (h // 2) * LANES:(h // 2 + 1) * LANES]
            s = _dot(jnp.concatenate([k_pair, c_tile], axis=1), w_ref[h, :, cols])
            if masked:
                kidx = lax.broadcasted_iota(jnp.int32, (t, tc), 0)
                qidx = lax.broadcasted_iota(jnp.int32, (t, tc), 1) + c * tc
                s = jnp.where(kidx <= qidx, s, MASK_VALUE)
            s_ref[h, :, cols] = s
            mx_ref[h, :, cols] = jnp.max(s, axis=0, keepdims=True)

    ones = jnp.ones((ONES_ROWS, t), BF16)

    def update(kt, src, which=pieces):
        s_ref, mx_ref = src
        ks = pl.multiple_of(kt * t, t)
        for h, c in which:
            cols = slice(c * tc, (c + 1) * tc)
            vt = vt_ref[0, 0, h * HEAD_DIM:(h + 1) * HEAD_DIM, pl.ds(ks, t)].astype(BF16)
            m_prev = m_ref[h, :, cols]
            m_new = jnp.maximum(m_prev, mx_ref[h, :, cols])
            alpha = jnp.exp2(m_prev - m_new)
            p = jnp.exp2(s_ref[h, :, cols] - m_new).astype(BF16)
            acc_ref[h, :, cols] = alpha * acc_ref[h, :, cols] + _dot(jnp.concatenate([vt, ones], axis=0), p)
            m_ref[h, :, cols] = m_new

    even_buf = (sa_ref, mxa_ref)
    odd_buf = (sb_ref, mxb_ref)

    def step(kt, masked_next):
        even = kt % 2 == 0

        def both(cur, nxt):
            for piece in pieces:
                scores(kt + 1, nxt, masked_next, [piece])
                update(kt, cur, [piece])

        @pl.when(even)
        def _():
            both(even_buf, odd_buf)

        @pl.when(jnp.logical_not(even))
        def _():
            both(odd_buf, even_buf)

    @pl.when(qi == 0)
    def _():
        scores(0, even_buf, True)

    @pl.when(qi > 0)
    def _():
        scores(0, even_buf, False)

        def body(kt, carry):
            step(kt, False)
            return carry

        lax.fori_loop(0, qi - 1, body, 0)
        step(qi - 1, True)

    @pl.when(qi % 2 == 0)
    def _():
        update(qi, even_buf)

    @pl.when(qi % 2 == 1)
    def _():
        update(qi, odd_buf)

    o = jnp.concatenate([acc_ref[h, 0:HEAD_DIM] / acc_ref[h, HEAD_DIM:HEAD_DIM + 1] for h in range(heads)],
                        axis=0)
    o_ref[0] = o.T


def _fox_attn(qt, kbf, caug, vt_stack, t, heads, chunks):
    n_seq, _, seq = qt.shape
    slot = vt_stack.shape[0] - 1
    width = heads * HEAD_DIM
    return pl.pallas_call(
        functools.partial(_fox_attn_kernel, t=t, heads=heads, chunks=chunks),
        grid=(n_seq, N_HEADS // heads, seq // t),
        in_specs=[pl.BlockSpec((1, width, t), lambda b, g, i: (b, g, i)),
                  pl.BlockSpec((1, seq, width), lambda b, g, i: (b, 0, g)),
                  pl.BlockSpec((1, seq, LANES), lambda b, g, i: (b, 0, 0)),
                  pl.BlockSpec((1, 1, width, seq), lambda b, g, i: (slot, b, g, 0))],
        out_specs=pl.BlockSpec((1, t, width), lambda b, g, i: (b, i, g)),
        out_shape=jax.ShapeDtypeStruct((n_seq, seq, 1024), F32),
        scratch_shapes=[pltpu.VMEM((heads, 2 * LANES, t), BF16),
                        pltpu.VMEM((heads, t, t), F32),
                        pltpu.VMEM((heads, t, t), F32),
                        pltpu.VMEM((heads, 1, t), F32),
                        pltpu.VMEM((heads, 1, t), F32),
                        pltpu.VMEM((heads, 1, t), F32),
                        pltpu.VMEM((heads, HEAD_DIM + ONES_ROWS, t), F32)],
        compiler_params=_params("arbitrary", "arbitrary", "arbitrary"),
        name="fox_attn",
    )(qt, kbf, caug, vt_stack)


def _fox_decode_kernel(pt_ref, q_ref, kn_ref, vn_ref, lfn_ref, tri_ref, *rest, n_new, group):
    del pt_ref
    k_refs, v_refs, lf_refs = rest[:group], rest[group:2 * group], rest[2 * group:3 * group]
    o_ref, qbd_ref, kbf_ref, vbf_ref, m_ref, l_ref, acc_ref, r_ref = rest[3 * group:]
    step = pl.program_id(1)
    n_rows = n_new * N_HEADS
    row = lax.broadcasted_iota(jnp.int32, (n_rows, 1024), 0)
    lane = lax.broadcasted_iota(jnp.int32, (n_rows, 1024), 1)
    own_head = (lane >> 6) == (row & (N_HEADS - 1))

    def lane_sums(lf, tri):
        out = _dot(jnp.concatenate(_split3(lf), axis=0), tri)
        return out[0:N_HEADS] + out[N_HEADS:2 * N_HEADS] + out[2 * N_HEADS:]

    def widen(x, reps):
        return jnp.concatenate([x] * reps, axis=1)

    @pl.when(step == 0)
    def _():
        q = q_ref[0].astype(F32)
        qb = jnp.concatenate([jnp.broadcast_to(q[t:t + 1], (N_HEADS, 1024)) for t in range(n_new)], axis=0)
        qb = jnp.where(own_head, qb, 0.0)
        qbd_ref[...] = qb.astype(BF16)
        rr = lax.broadcasted_iota(jnp.int32, (LANES, LANES), 0)
        cc = lax.broadcasted_iota(jnp.int32, (LANES, LANES), 1)
        incl = jnp.where(rr <= cc, 1.0, 0.0).astype(BF16)
        bias = -LOG2E * lane_sums(lfn_ref[0], incl)
        bias = jnp.concatenate([bias] * n_new, axis=0)
        j_idx = lax.broadcasted_iota(jnp.int32, (n_rows, LANES), 1)
        t_idx = lax.broadcasted_iota(jnp.int32, (n_rows, LANES), 0) >> 4
        kn = kn_ref[0]
        vn = vn_ref[0]
        s = jnp.full((n_rows, LANES), MASK_VALUE, F32)
        for j in range(n_new):
            sj = jnp.sum(qb * kn[j:j + 1, :], axis=-1, keepdims=True)
            s = jnp.where(j_idx == j, sj, s)
        s = jnp.where(j_idx <= t_idx, s + bias, MASK_VALUE)
        m = jnp.max(s, axis=-1, keepdims=True)
        p = jnp.exp2(s - m)
        m_ref[...] = jnp.broadcast_to(m, (n_rows, LANES))
        l_ref[...] = jnp.broadcast_to(jnp.sum(p, axis=-1, keepdims=True), (n_rows, LANES))
        acc = jnp.zeros((n_rows, 1024), F32)
        for j in range(n_new):
            acc = acc + p[:, j:j + 1] * vn[j:j + 1, :]
        acc_ref[...] = acc
        r_ref[...] = jnp.zeros_like(r_ref)

    r = r_ref[...]
    biases = []
    for i in range(group):
        kbf_ref[:, i * LANES:(i + 1) * LANES] = k_refs[i][0, 0].astype(BF16)
        vbf_ref[:, i * LANES:(i + 1) * LANES] = v_refs[i][0, 0].astype(BF16)
        lf = lf_refs[i][0, 0]
        biases.append(lane_sums(lf, tri_ref[...]) + r)
        r = r + jnp.sum(lf, axis=-1, keepdims=True)
    r_ref[...] = r
    bias = LOG2E * jnp.concatenate(biases, axis=1)
    s = _dot(qbd_ref[...], kbf_ref[...]) + jnp.concatenate([bias] * n_new, axis=0)
    m_prev = m_ref[...]
    m_new = jnp.maximum(m_prev, jnp.max(s, axis=-1, keepdims=True))
    alpha = jnp.exp2(m_prev - m_new)
    p = jnp.exp2(s - widen(m_new, group))
    l_ref[...] = alpha * l_ref[...] + jnp.sum(p, axis=-1, keepdims=True)
    acc_ref[...] = widen(alpha, 1024 // LANES) * acc_ref[...] + _dot_nt(p.astype(BF16), vbf_ref[...])
    m_ref[...] = m_new

    @pl.when(step == pl.num_programs(1) - 1)
    def _():
        o = jnp.where(own_head, acc_ref[...] / widen(l_ref[...], 1024 // LANES), 0.0)
        for t in range(n_new):
            o_ref[0, t:t + 1, :] = jnp.sum(o[t * N_HEADS:(t + 1) * N_HEADS], axis=0, keepdims=True)


def _fox_decode(page_table, q, kn, vn, lfn, cache_kt, cache_vt, cache_lft, layer, group):
    n_b, n_pages = page_table.shape
    n_new = q.shape[1]
    rr = lax.broadcasted_iota(jnp.int32, (LANES, LANES), 0)
    cc = lax.broadcasted_iota(jnp.int32, (LANES, LANES), 1)
    strict = jnp.where(rr > cc, 1.0, 0.0).astype(BF16)

    def page_spec(shape, i):
        return pl.BlockSpec((1, 1) + shape,
                            lambda b, s, pt: (layer, pt[b, n_pages - 1 - (s * group + i)], 0, 0))

    per_b = lambda shape: pl.BlockSpec((1,) + shape, lambda b, s, pt: (b,) + (0,) * len(shape))
    n_rows = n_new * N_HEADS
    grid_spec = pltpu.PrefetchScalarGridSpec(
        num_scalar_prefetch=1,
        grid=(n_b, n_pages // group),
        in_specs=([per_b((n_new, 1024)), per_b((n_new, 1024)), per_b((n_new, 1024)), per_b((N_HEADS, LANES)),
                   pl.BlockSpec((LANES, LANES), lambda b, s, pt: (0, 0))]
                  + [page_spec((1024, LANES), i) for i in range(group)]
                  + [page_spec((1024, LANES), i) for i in range(group)]
                  + [page_spec((N_HEADS, LANES), i) for i in range(group)]),
        out_specs=per_b((n_new, 1024)),
        scratch_shapes=[pltpu.VMEM((n_rows, 1024), BF16),
                        pltpu.VMEM((1024, LANES * group), BF16),
                        pltpu.VMEM((1024, LANES * group), BF16),
                        pltpu.VMEM((n_rows, LANES), F32),
                        pltpu.VMEM((n_rows, LANES), F32),
                        pltpu.VMEM((n_rows, 1024), F32),
                        pltpu.VMEM((N_HEADS, LANES), F32)])
    return pl.pallas_call(
        functools.partial(_fox_decode_kernel, n_new=n_new, group=group),
        grid_spec=grid_spec,
        out_shape=jax.ShapeDtypeStruct((n_b, n_new, 1024), F32),
        compiler_params=_params("arbitrary", "arbitrary"),
        name="fox_decode",
    )(page_table, q, kn, vn, lfn, strict,
      *([cache_kt] * group), *([cache_vt] * group), *([cache_lft] * group))


def _fox_out_kernel(o_ref, g_ref, x_ref, w_ref, lng_ref, lnb_ref, y_ref):
    a = (o_ref[...] * _silu(g_ref[...])).astype(BF16)
    z = DN_ALPHA * x_ref[...] + _dot(a, w_ref[...])
    y_ref[...] = _layer_norm(z, lng_ref[...], lnb_ref[...])


def _fox_out(o2d, g2d, x2d, w_out, ln_g, ln_b, tm):
    m = x2d.shape[0]
    wide = pl.BlockSpec((tm, 1024), lambda i: (i, 0))
    return pl.pallas_call(
        _fox_out_kernel,
        grid=(m // tm,),
        in_specs=[wide, wide, wide, _const_spec((1024, 1024)), _const_spec((1, 1024)),
                  _const_spec((1, 1024))],
        out_specs=wide,
        out_shape=jax.ShapeDtypeStruct((m, 1024), F32),
        compiler_params=_params("arbitrary"),
        name="fox_out",
    )(o2d, g2d, x2d, w_out, ln_g, ln_b)


def _pool_tail(m_bf, gate, x, wg_ref, bg_ref, sc_ref, wo_ref, lng_ref, lnb_ref):
    z = jnp.concatenate([_dot(m_bf[:, g * GROUP_B:(g + 1) * GROUP_B], wg_ref[g])
                         for g in range(len(POOL_WINDOWS))], axis=1) + bg_ref[...]
    a = (z * sc_ref[...] * _silu(gate)).astype(BF16)
    zz = DN_ALPHA * x + _dot(a, wo_ref[...])
    return _layer_norm(zz, lng_ref[...], lnb_ref[...])


def _pool_prompt_kernel(x_ref, wi_ref, wg_ref, bg_ref, sc_ref, wo_ref, lng_ref, lnb_ref,
                        y_ref, hist_ref, ext_ref, m_ref, *, tm):
    i = pl.program_id(1)

    @pl.when(i == 0)
    def _():
        ext_ref[0:HIST_PAD, :] = jnp.zeros((HIST_PAD, WIDTH_B), F32)

    x = x_ref[...]
    xb = x.astype(BF16)
    a = _dot(xb, wi_ref[:, 0:WIDTH_B])
    ext_ref[HIST_PAD:, :] = a
    for g, w in enumerate(POOL_WINDOWS):
        cols = slice(g * GROUP_B, (g + 1) * GROUP_B)
        win = a[:, cols]
        for d in range(1, w):
            win = win + ext_ref[HIST_PAD - d:HIST_PAD - d + tm, cols]
        m_ref[:, cols] = (win * (1.0 / w) - a[:, cols]).astype(BF16)

    @pl.when(i == 0)
    def _():
        pos = lax.broadcasted_iota(jnp.int32, (HIST_PAD, GROUP_B), 0)
        for g, w in enumerate(POOL_WINDOWS):
            cols = slice(g * GROUP_B, (g + 1) * GROUP_B)
            win = ext_ref[HIST_PAD:2 * HIST_PAD, cols]
            for d in range(1, w):
                win = win + ext_ref[HIST_PAD - d:2 * HIST_PAD - d, cols]
            cnt = jnp.minimum(pos + 1, w).astype(F32)
            m_ref[0:HIST_PAD, cols] = (win / cnt - ext_ref[HIST_PAD:2 * HIST_PAD, cols]).astype(BF16)

    gate = _dot(xb, wi_ref[:, WIDTH_B:])
    y_ref[...] = _pool_tail(m_ref[...], gate, x, wg_ref, bg_ref, sc_ref, wo_ref, lng_ref, lnb_ref)
    hist_ref[0] = a[tm - HIST_PAD:, :]
    ext_ref[0:HIST_PAD, :] = a[tm - HIST_PAD:, :]


def _pool_prompt(x2d, w_in, w_grp, b_grp, scale, w_out, ln_g, ln_b, n_seq, tm):
    m = x2d.shape[0]
    nt = m // (n_seq * tm)
    wide = pl.BlockSpec((tm, 1024), lambda b, i: (b * nt + i, 0))
    return pl.pallas_call(
        functools.partial(_pool_prompt_kernel, tm=tm),
        grid=(n_seq, nt),
        in_specs=[wide, _const_spec((1024, 2 * WIDTH_B)), _const_spec((4, GROUP_B, GROUP_B)),
                  _const_spec((1, WIDTH_B)), _const_spec((1, WIDTH_B)), _const_spec((WIDTH_B, 1024)),
                  _const_spec((1, 1024)), _const_spec((1, 1024))],
        out_specs=[wide, pl.BlockSpec((1, HIST_PAD, WIDTH_B), lambda b, i: (b, 0, 0))],
        out_shape=[jax.ShapeDtypeStruct((m, 1024), F32),
                   jax.ShapeDtypeStruct((n_seq, HIST_PAD, WIDTH_B), F32)],
        scratch_shapes=[pltpu.VMEM((HIST_PAD + tm, WIDTH_B), F32),
                        pltpu.VMEM((tm, WIDTH_B), BF16)],
        compiler_params=_params("arbitrary", "arbitrary"),
        name="pool_prompt",
    )(x2d, w_in, w_grp, b_grp, scale, w_out, ln_g, ln_b)


def _pool_sample_kernel(x_ref, hist_ref, wi_ref, wg_ref, bg_ref, sc_ref, wo_ref, lng_ref, lnb_ref,
                        y_ref, newhist_ref, *, n_new):
    xs = [x_ref[t] for t in range(n_new)]
    a = [_dot(x.astype(BF16), wi_ref[:, 0:WIDTH_B]) for x in xs]
    ext = [hist_ref[r] for r in range(POOL_HIST)] + a
    for r in range(POOL_HIST):
        newhist_ref[r] = ext[n_new + r]
    for t in range(n_new):
        parts = []
        for g, w in enumerate(POOL_WINDOWS):
            cols = slice(g * GROUP_B, (g + 1) * GROUP_B)
            win = a[t][:, cols]
            for d in range(1, w):
                win = win + ext[POOL_HIST + t - d][:, cols]
            parts.append(win * (1.0 / w) - a[t][:, cols])
        m_bf = jnp.concatenate(parts, axis=1).astype(BF16)
        gate = _dot(xs[t].astype(BF16), wi_ref[:, WIDTH_B:])
        y_ref[t] = _pool_tail(m_bf, gate, xs[t], wg_ref, bg_ref, sc_ref, wo_ref, lng_ref, lnb_ref)


def _pool_sample(x_tm, hist_tm, w_in, w_grp, b_grp, scale, w_out, ln_g, ln_b):
    n_new, n_b, _ = x_tm.shape
    full = lambda a: _const_spec(a.shape)
    args = (x_tm, hist_tm, w_in, w_grp, b_grp, scale, w_out, ln_g, ln_b)
    return pl.pallas_call(
        functools.partial(_pool_sample_kernel, n_new=n_new),
        grid=(1,),
        in_specs=[full(a) for a in args],
        out_specs=[_const_spec((n_new, n_b, 1024)), _const_spec((POOL_HIST, n_b, WIDTH_B))],
        out_shape=[jax.ShapeDtypeStruct((n_new, n_b, 1024), F32),
                   jax.ShapeDtypeStruct((POOL_HIST, n_b, WIDTH_B), F32)],
        compiler_params=_params("arbitrary"),
        name="pool_sample",
    )(*args)


def _gmlp_kernel(x_ref, wi_ref, vg_ref, vb_ref, ws_ref, bs_ref, wo_ref, lng_ref, lnb_ref,
                 *out_refs, tm, span, emit_v):
    y_ref = out_refs[0]
    x = x_ref[...]
    xb = x.astype(BF16)
    v = _layer_norm(_dot(xb, wi_ref[:, WIDTH_C:2 * WIDTH_C]), vg_ref[...], vb_ref[...])
    if emit_v:
        out_refs[1][...] = v
    vb16 = v.astype(BF16)
    r = lax.broadcasted_iota(jnp.int32, (CHUNK, CHUNK), 0)
    c = lax.broadcasted_iota(jnp.int32, (CHUNK, CHUNK), 1)
    keep = (c <= r) & ((r ^ c) < span)
    chunks = []
    for n in range(tm // CHUNK):
        rows = slice(n * CHUNK, (n + 1) * CHUNK)
        groups = []
        for g in range(N_GROUPS_C):
            wmix = jnp.where(keep, ws_ref[g], 0.0).astype(BF16)
            groups.append(_dot(wmix, vb16[rows, g * GROUP_C:(g + 1) * GROUP_C]))
        chunks.append(jnp.concatenate(groups, axis=1) + bs_ref[...])
    sv = jnp.concatenate(chunks, axis=0) if len(chunks) > 1 else chunks[0]
    u = _dot(xb, wi_ref[:, 0:WIDTH_C])
    gate = _dot(xb, wi_ref[:, 2 * WIDTH_C:])
    a = (u * sv * _silu(gate)).astype(BF16)
    z = DN_ALPHA * x + _dot(a, wo_ref[...])
    y_ref[...] = _layer_norm(z, lng_ref[...], lnb_ref[...])


def _gmlp(x2d, w_in, vg, vb, ws_tiled, bs_rows, w_out, ln_g, ln_b, tm, span, emit_v):
    m = x2d.shape[0]
    wide = pl.BlockSpec((tm, 1024), lambda i: (i, 0))
    out_specs = [wide]
    out_shape = [jax.ShapeDtypeStruct((m, 1024), F32)]
    if emit_v:
        out_specs.append(pl.BlockSpec((tm, WIDTH_C), lambda i: (i, 0)))
        out_shape.append(jax.ShapeDtypeStruct((m, WIDTH_C), F32))
    return pl.pallas_call(
        functools.partial(_gmlp_kernel, tm=tm, span=span, emit_v=emit_v),
        grid=(m // tm,),
        in_specs=[wide, _const_spec((1024, 3 * WIDTH_C)), _const_spec((1, WIDTH_C)),
                  _const_spec((1, WIDTH_C)), _const_spec((N_GROUPS_C, CHUNK, CHUNK)),
                  _const_spec((CHUNK, WIDTH_C)), _const_spec((WIDTH_C, 1024)),
                  _const_spec((1, 1024)), _const_spec((1, 1024))],
        out_specs=out_specs,
        out_shape=out_shape,
        compiler_params=_params("arbitrary"),
        name="gmlp_v" if emit_v else "gmlp",
    )(x2d, w_in, vg, vb, ws_tiled, bs_rows, w_out, ln_g, ln_b)


def _row(v):
    return v.reshape(1, -1)


def _fox_layer(xp, xs, prev_kt, prev_vt, cache_kt, cache_vt, cache_lft, page_table, layer,
               w_in, b_f, w_out, ln_g, ln_b):
    n_seq, seq, _ = xp.shape
    n_b, n_new, _ = xs.shape
    w_main = w_in[:, :4 * 1024].astype(BF16)
    w_f = jnp.pad(jnp.tile(w_in[:, 4 * 1024:], (1, 3)), ((0, 0), (0, LANES - 3 * N_HEADS))).astype(BF16)
    b_fp = jnp.pad(jnp.tile(b_f, 3), (0, LANES - 3 * N_HEADS)).reshape(1, LANES)
    w_o = w_out.astype(BF16)

    x2d = xp.reshape(n_seq * seq, 1024)
    qt, kt32, vt32, kbf, gate, logf, caug = _fox_proj_prompt(
        x2d, w_main[:, :3 * 1024].T, w_main[:, 1024:2048], w_main[:, 3 * 1024:], w_f, b_fp,
        prev_kt, prev_vt, n_seq, 512)
    o = _fox_attn(qt, kbf.reshape(n_seq, seq, 1024), caug.reshape(n_seq, seq, LANES), vt32, 512, 4, 1)
    yp = _fox_out(o.reshape(n_seq * seq, 1024), gate, x2d, w_o, ln_g, ln_b, 512)
    out_p = (kt32, vt32, logf.reshape(n_seq, seq, N_HEADS))

    rows = n_b * n_new
    xs2d = xs.reshape(rows, 1024)
    qs, ks32, vs32, gs, lfs = _fox_proj_rows(xs2d, w_main, w_f, b_fp)
    lfn = jnp.pad(lfs.reshape(n_b, n_new, N_HEADS).transpose(0, 2, 1),
                  ((0, 0), (0, 0), (0, LANES - n_new)))
    od = _fox_decode(page_table, qs.reshape(n_b, n_new, 1024), ks32.reshape(n_b, n_new, 1024),
                     vs32.reshape(n_b, n_new, 1024), lfn, cache_kt, cache_vt, cache_lft, layer, 16)
    ys = _fox_out(od.reshape(rows, 1024), gs, xs2d, w_o, ln_g, ln_b, rows)
    out_s = (ks32.reshape(n_b, n_new, N_HEADS, HEAD_DIM), vs32.reshape(n_b, n_new, N_HEADS, HEAD_DIM),
             lfs.reshape(n_b, n_new, N_HEADS))
    return yp.reshape(n_seq, seq, 1024), ys.reshape(n_b, n_new, 1024), out_p, out_s


def _pool_layer(xp, xs, hist, w_in, w_grp, b_grp, scale, w_out, ln_g, ln_b):
    n_seq, seq, _ = xp.shape
    n_b, n_new, _ = xs.shape
    weights = (w_in.astype(BF16), w_grp.astype(BF16), _row(b_grp), _row(scale), w_out.astype(BF16),
               ln_g, ln_b)
    yp, hp = _pool_prompt(xp.reshape(n_seq * seq, 1024), *weights, n_seq, 256)
    ys, hs = _pool_sample(xs.transpose(1, 0, 2), hist.transpose(1, 0, 2), *weights)
    return (yp.reshape(n_seq, seq, 1024), ys.transpose(1, 0, 2),
            hp[:, HIST_PAD - POOL_HIST:, :], hs.transpose(1, 0, 2))


def _gmlp_layer(xp, xs, w_in, vg, vb, w_s, b_s, w_out, ln_g, ln_b):
    n_seq, seq, _ = xp.shape
    n_b, n_new, _ = xs.shape
    rows = n_b * n_new
    shared = (w_in.astype(BF16), _row(vg), _row(vb))
    tail = (w_out.astype(BF16), ln_g, ln_b)
    bs_p = jnp.repeat(b_s.T, GROUP_C, axis=1)
    yp, = _gmlp(xp.reshape(n_seq * seq, 1024), *shared, w_s, bs_p, *tail, 256, CHUNK, False)
    reps = CHUNK // n_new
    ws_s = jnp.tile(w_s[:, :n_new, :n_new], (1, reps, reps))
    bs_s = jnp.tile(jnp.repeat(b_s[:, :n_new].T, GROUP_C, axis=1), (reps, 1))
    ys, vrows = _gmlp(xs.reshape(rows, 1024), *shared, ws_s, bs_s, *tail, rows, n_new, True)
    return yp.reshape(n_seq, seq, 1024), ys.reshape(n_b, n_new, 1024), vrows.reshape(n_b, n_new, WIDTH_C)


def _rows_major(t):
    return t.reshape(t.shape[0], t.shape[1], N_HEADS, HEAD_DIM, t.shape[3]).transpose(0, 1, 4, 2, 3)


def kernel(x_prompt, x_sample, cache_k, cache_v, cache_logf, state_pool, page_table, ln_g, ln_b, w_in_a, b_f_a, w_out_a, w_in_b, w_grp_b, b_grp_b, scale_b, w_out_b, w_in_c, ln_v_g, ln_v_b, w_s_c, b_s_c, w_out_c):
    n_layers_a, n_pool, page_size = cache_k.shape[:3]
    ck = cache_k.transpose(0, 1, 3, 4, 2).reshape(n_layers_a, n_pool, N_HEADS * HEAD_DIM, page_size)
    cv = cache_v.transpose(0, 1, 3, 4, 2).reshape(n_layers_a, n_pool, N_HEADS * HEAD_DIM, page_size)
    clft = jnp.swapaxes(cache_logf, 2, 3)
    xp, xs = x_prompt, x_sample
    kt_stack, vt_stack = None, None
    fp_l, ks_l, vs_l, fs_l = [], [], [], []
    poolp_l, pools_l, chunkv_l = [], [], []
    for i in range(DEPTH):
        j = i // N_MIXERS
        kind = i % N_MIXERS
        g, b = _row(ln_g[i]), _row(ln_b[i])
        if kind == 0:
            xp, xs, out_p, out_s = _fox_layer(xp, xs, kt_stack, vt_stack, ck, cv, clft, page_table, j,
                                              w_in_a[j], b_f_a[j], w_out_a[j], g, b)
            kt_stack, vt_stack = out_p[0], out_p[1]
            fp_l.append(out_p[2])
            ks_l.append(out_s[0]); vs_l.append(out_s[1]); fs_l.append(out_s[2])
        elif kind == 1:
            xp, xs, hp, hs = _pool_layer(xp, xs, state_pool[j], w_in_b[j], w_grp_b[j], b_grp_b[j],
                                         scale_b[j], w_out_b[j], g, b)
            poolp_l.append(hp); pools_l.append(hs)
        else:
            xp, xs, vrows = _gmlp_layer(xp, xs, w_in_c[j], ln_v_g[j], ln_v_b[j], w_s_c[j], b_s_c[j],
                                        w_out_c[j], g, b)
            chunkv_l.append(vrows)
    return (xp, xs,
            _rows_major(kt_stack), _rows_major(vt_stack), jnp.stack(fp_l),
            jnp.stack(ks_l), jnp.stack(vs_l), jnp.stack(fs_l),
            jnp.stack(poolp_l), jnp.stack(pools_l),
            jnp.stack(chunkv_l))
```

```python
import functools

import jax
import jax.numpy as jnp
from jax import lax
from jax.experimental import pallas as pl
from jax.experimental.pallas import tpu as pltpu

D_MODEL = 1024
DEPTH = 4
N_MIXERS = 3
N_HEADS = 16
HEAD_DIM = 64
ATTN_SCALE = HEAD_DIM ** -0.5
WIDTH_B = 2048
POOL_WINDOWS = (2, 4, 8, 16)
GROUP_B = 512
POOL_HIST = 15
WIDTH_C = 2048
CHUNK = 128
N_GROUPS_C = 4
GROUP_C = 512
DN_ALPHA = (2 * DEPTH) ** 0.25
LN_EPS = 1e-5
LOG2E = 1.4426950408889634

LANES = 128
HIST_PAD = 16
ONES_ROWS = 16
MASK_VALUE = -1e30
VMEM_LIMIT_BYTES = 56 * 1024 * 1024

F32 = jnp.float32
BF16 = jnp.bfloat16


def _params(*sem):
    return pltpu.CompilerParams(dimension_semantics=sem, vmem_limit_bytes=VMEM_LIMIT_BYTES)


def _dot(a, b):
    return jnp.dot(a, b, preferred_element_type=F32)


def _dot_nt(a, b):
    return lax.dot_general(a, b, (((1,), (1,)), ((), ())), preferred_element_type=F32)


def _silu(x):
    return x / (1.0 + jnp.exp(-x))


def _log_sigmoid(f):
    return jnp.minimum(f, 0.0) - jnp.log1p(jnp.exp(-jnp.abs(f)))


def _layer_norm(z, g, b):
    mu = jnp.mean(z, axis=-1, keepdims=True)
    d = z - mu
    var = jnp.mean(d * d, axis=-1, keepdims=True)
    return d * lax.rsqrt(var + LN_EPS) * g + b


def _split3(x):
    hi = x.astype(BF16)
    r1 = x - hi.astype(F32)
    mid = r1.astype(BF16)
    lo = (r1 - mid.astype(F32)).astype(BF16)
    return hi, mid, lo


def _const_spec(shape):
    return pl.BlockSpec(shape, lambda *_: (0,) * len(shape))


def _fox_proj_prompt_kernel(x_ref, wt_ref, wk_ref, wg_ref, wf_ref, bf_ref, tri_ref, *rest, n_prev):
    if n_prev:
        pk_ref, pv_ref = rest[:2]
        rest = rest[2:]
    qt_ref, kt_ref, vt_ref, kbf_ref, g_ref, logf_ref, caug_ref, carry_ref = rest

    @pl.when(pl.program_id(1) == 0)
    def _():
        carry_ref[...] = jnp.zeros_like(carry_ref)

    xb = x_ref[...].astype(BF16)
    qt_ref[0] = (_dot_nt(wt_ref[0:1024, :], xb) * (ATTN_SCALE * LOG2E)).astype(BF16)
    kt_ref[n_prev, 0] = _dot_nt(wt_ref[1024:2048, :], xb)
    vt_ref[n_prev, 0] = _dot_nt(wt_ref[2048:3072, :], xb)
    for slot in range(n_prev):
        kt_ref[slot, 0] = pk_ref[slot, 0]
        vt_ref[slot, 0] = pv_ref[slot, 0]
    kbf_ref[...] = _dot(xb, wk_ref[...]).astype(BF16)
    g_ref[...] = _dot(xb, wg_ref[...])

    lf = _log_sigmoid(_dot(xb, wf_ref[...]) + bf_ref[...])
    logf_ref[...] = lf[:, :N_HEADS]
    hi, mid, lo = _split3(lf)
    tri = tri_ref[...]
    c = _dot(tri, hi) + _dot(tri, mid) + _dot(tri, lo) + carry_ref[...]
    carry_ref[...] = c[c.shape[0] - 1:, :]
    neg = -LOG2E * c
    nhi = neg.astype(BF16).astype(F32)
    rem = neg - nhi
    nmid = rem.astype(BF16).astype(F32)
    lane = lax.broadcasted_iota(jnp.int32, c.shape, 1)
    terms = jnp.where(lane < N_HEADS, nhi,
                      jnp.where(lane < 2 * N_HEADS, nmid,
                                jnp.where(lane < 3 * N_HEADS, rem - nmid, 0.0)))
    caug_ref[...] = terms.astype(BF16)


def _fox_proj_prompt(x2d, w_t, w_k, w_g, w_f, b_f, prev_kt, prev_vt, n_seq, tm):
    m = x2d.shape[0]
    seq = m // n_seq
    nt = seq // tm
    n_prev = 0 if prev_kt is None else prev_kt.shape[0]
    tri = jnp.tril(jnp.ones((tm, tm), BF16))
    const = lambda shape: pl.BlockSpec(shape, lambda b, i: (0,) * len(shape), pipeline_mode=pl.Buffered(1))
    row = lambda b, i: (b * nt + i, 0)
    wide = pl.BlockSpec((tm, 1024), row)
    tall = pl.BlockSpec((1, 1024, tm), lambda b, i: (b, 0, i))
    stacked = pl.BlockSpec((n_prev + 1, 1, 1024, tm), lambda b, i: (0, b, 0, i))
    prev = pl.BlockSpec((n_prev, 1, 1024, tm), lambda b, i: (0, b, 0, i))
    stack_shape = jax.ShapeDtypeStruct((n_prev + 1, n_seq, 1024, seq), F32)
    return pl.pallas_call(
        functools.partial(_fox_proj_prompt_kernel, n_prev=n_prev),
        grid=(n_seq, nt),
        in_specs=[wide, const((3072, 1024)), const((1024, 1024)), const((1024, 1024)),
                  const((1024, LANES)), const((1, LANES)), const((tm, tm))] + [prev, prev] * bool(n_prev),
        out_specs=[tall, stacked, stacked, wide, wide, pl.BlockSpec((tm, N_HEADS), row),
                   pl.BlockSpec((tm, LANES), row)],
        out_shape=[jax.ShapeDtypeStruct((n_seq, 1024, seq), BF16),
                   stack_shape,
                   stack_shape,
                   jax.ShapeDtypeStruct((m, 1024), BF16),
                   jax.ShapeDtypeStruct((m, 1024), F32),
                   jax.ShapeDtypeStruct((m, N_HEADS), F32),
                   jax.ShapeDtypeStruct((m, LANES), BF16)],
        scratch_shapes=[pltpu.VMEM((1, LANES), F32)],
        compiler_params=_params("arbitrary", "arbitrary"),
        name="fox_proj_prompt",
    )(x2d, w_t, w_k, w_g, w_f, b_f, tri, *([prev_kt, prev_vt] * bool(n_prev)))


def _fox_proj_rows_kernel(x_ref, w_ref, wf_ref, bf_ref, q_ref, k_ref, v_ref, g_ref, logf_ref):
    xb = x_ref[...].astype(BF16)
    q_ref[...] = (_dot(xb, w_ref[:, 0:1024]) * (ATTN_SCALE * LOG2E)).astype(BF16)
    k_ref[...] = _dot(xb, w_ref[:, 1024:2048])
    v_ref[...] = _dot(xb, w_ref[:, 2048:3072])
    g_ref[...] = _dot(xb, w_ref[:, 3072:4096])
    lf = _log_sigmoid(_dot(xb, wf_ref[...]) + bf_ref[...])
    logf_ref[...] = lf[:, :N_HEADS]


def _fox_proj_rows(x2d, w_main, w_f, b_f):
    m = x2d.shape[0]
    wide = _const_spec((m, 1024))
    return pl.pallas_call(
        _fox_proj_rows_kernel,
        grid=(1,),
        in_specs=[wide, _const_spec((1024, 4096)), _const_spec((1024, LANES)), _const_spec((1, LANES))],
        out_specs=[wide, wide, wide, wide, _const_spec((m, N_HEADS))],
        out_shape=[jax.ShapeDtypeStruct((m, 1024), BF16),
                   jax.ShapeDtypeStruct((m, 1024), F32),
                   jax.ShapeDtypeStruct((m, 1024), F32),
                   jax.ShapeDtypeStruct((m, 1024), F32),
                   jax.ShapeDtypeStruct((m, N_HEADS), F32)],
        compiler_params=_params("arbitrary"),
        name="fox_proj_rows",
    )(x2d, w_main, w_f, b_f)


def _fox_attn_kernel(qt_ref, k_ref, caug_ref, vt_ref, o_ref,
                     w_ref, sa_ref, sb_ref, mxa_ref, mxb_ref, m_ref, acc_ref, *, t, heads, chunks):
    group = pl.program_id(1)
    qi = pl.program_id(2)
    row = lax.broadcasted_iota(jnp.int32, (LANES, t), 0)
    z64 = jnp.zeros((HEAD_DIM, t), BF16)
    for h in range(heads):
        head = heads * group + h
        sel = (row == head) | (row == head + N_HEADS) | (row == head + 2 * N_HEADS)
        sel = jnp.where(sel, 1.0, 0.0).astype(BF16)
        q_h = qt_ref[0, h * HEAD_DIM:(h + 1) * HEAD_DIM, :]
        w_ref[h] = jnp.concatenate(([q_h, z64] if h % 2 == 0 else [z64, q_h]) + [sel], axis=0)
    m_ref[...] = jnp.full_like(m_ref, MASK_VALUE)
    acc_ref[...] = jnp.zeros_like(acc_ref)

    tc = t // chunks
    pieces = tuple((h, c) for h in range(heads) for c in range(chunks))

    def scores(kt, dst, masked, which=pieces):
        s_ref, mx_ref = dst
        ks = pl.multiple_of(kt * t, t)
        c_tile = caug_ref[0, pl.ds(ks, t), :]
        for h, c in which:
            cols = slice(c * tc, (c + 1) * tc)
            k_pair = k_ref[0, pl.ds(ks, t), (h // 2) * LANES:(h // 2 + 1) * LANES]
            s = _dot(jnp.concatenate([k_pair, c_tile], axis=1), w_ref[h, :, cols])
            if masked:
                kidx = lax.broadcasted_iota(jnp.int32, (t, tc), 0)
                qidx = lax.broadcasted_iota(jnp.int32, (t, tc), 1) + c * tc
                s = jnp.where(kidx <= qidx, s, MASK_VALUE)
            s_ref[h, :, cols] = s
            mx_ref[h, :, cols] = jnp.max(s, axis=0, keepdims=True)

    ones = jnp.ones((ONES_ROWS, t), BF16)

    def update(kt, src, which=pieces):
        s_ref, mx_ref = src
        ks = pl.multiple_of(kt * t, t)
        for h, c in which:
            cols = slice(c * tc, (c + 1) * tc)
            vt = vt_ref[0, 0, h * HEAD_DIM:(h + 1) * HEAD_DIM, pl.ds(ks, t)].astype(BF16)
            m_prev = m_ref[h, :, cols]
            m_new = jnp.maximum(m_prev, mx_ref[h, :, cols])
            alpha = jnp.exp2(m_prev - m_new)
            p = jnp.exp2(s_ref[h, :, cols] - m_new).astype(BF16)
            acc_ref[h, :, cols] = alpha * acc_ref[h, :, cols] + _dot(jnp.concatenate([vt, ones], axis=0), p)
            m_ref[h, :, cols] = m_new

    even_buf = (sa_ref, mxa_ref)
    odd_buf = (sb_ref, mxb_ref)

    def step(kt, masked_next):
        even = kt % 2 == 0

        def both(cur, nxt):
            for piece in pieces:
                scores(kt + 1, nxt, masked_next, [piece])
                update(kt, cur, [piece])

        @pl.when(even)
        def _():
            both(even_buf, odd_buf)

        @pl.when(jnp.logical_not(even))
        def _():
            both(odd_buf, even_buf)

    @pl.when(qi == 0)
    def _():
        scores(0, even_buf, True)

    @pl.when(qi > 0)
    def _():
        scores(0, even_buf, False)

        def body(kt, carry):
            step(kt, False)
            return carry

        lax.fori_loop(0, qi - 1, body, 0)
        step(qi - 1, True)

    @pl.when(qi % 2 == 0)
    def _():
        update(qi, even_buf)

    @pl.when(qi % 2 == 1)
    def _():
        update(qi, odd_buf)

    o = jnp.concatenate([acc_ref[h, 0:HEAD_DIM] / acc_ref[h, HEAD_DIM:HEAD_DIM + 1] for h in range(heads)],
                        axis=0)
    o_ref[0] = o.T


def _fox_attn(qt, kbf, caug, vt_stack, t, heads, chunks):
    n_seq, _, seq = qt.shape
    slot = vt_stack.shape[0] - 1
    width = heads * HEAD_DIM
    return pl.pallas_call(
        functools.partial(_fox_attn_kernel, t=t, heads=heads, chunks=chunks),
        grid=(n_seq, N_HEADS // heads, seq // t),
        in_specs=[pl.BlockSpec((1, width, t), lambda b, g, i: (b, g, i)),
                  pl.BlockSpec((1, seq, width), lambda b, g, i: (b, 0, g)),
                  pl.BlockSpec((1, seq, LANES), lambda b, g, i: (b, 0, 0)),
                  pl.BlockSpec((1, 1, width, seq), lambda b, g, i: (slot, b, g, 0))],
        out_specs=pl.BlockSpec((1, t, width), lambda b, g, i: (b, i, g)),
        out_shape=jax.ShapeDtypeStruct((n_seq, seq, 1024), F32),
        scratch_shapes=[pltpu.VMEM((heads, 2 * LANES, t), BF16),
                        pltpu.VMEM((heads, t, t), F32),
                        pltpu.VMEM((heads, t, t), F32),
                        pltpu.VMEM((heads, 1, t), F32),
                        pltpu.VMEM((heads, 1, t), F32),
                        pltpu.VMEM((heads, 1, t), F32),
                        pltpu.VMEM((heads, HEAD_DIM + ONES_ROWS, t), F32)],
        compiler_params=_params("arbitrary", "arbitrary", "arbitrary"),
        name="fox_attn",
    )(qt, kbf, caug, vt_stack)


def _fox_decode_kernel(pt_ref, q_ref, kn_ref, vn_ref, lfn_ref, tri_ref, *rest, n_new, group):
    del pt_ref
    k_refs, v_refs, lf_refs = rest[:group], rest[group:2 * group], rest[2 * group:3 * group]
    o_ref, qbd_ref, kbf_ref, vbf_ref, m_ref, l_ref, acc_ref, r_ref = rest[3 * group:]
    step = pl.program_id(1)
    n_rows = n_new * N_HEADS
    row = lax.broadcasted_iota(jnp.int32, (n_rows, 1024), 0)
    lane = lax.broadcasted_iota(jnp.int32, (n_rows, 1024), 1)
    own_head = (lane >> 6) == (row & (N_HEADS - 1))

    def lane_sums(lf, tri):
        out = _dot(jnp.concatenate(_split3(lf), axis=0), tri)
        return out[0:N_HEADS] + out[N_HEADS:2 * N_HEADS] + out[2 * N_HEADS:]

    def widen(x, reps):
        return jnp.concatenate([x] * reps, axis=1)

    @pl.when(step == 0)
    def _():
        q = q_ref[0].astype(F32)
        qb = jnp.concatenate([jnp.broadcast_to(q[t:t + 1], (N_HEADS, 1024)) for t in range(n_new)], axis=0)
        qb = jnp.where(own_head, qb, 0.0)
        qbd_ref[...] = qb.astype(BF16)
        rr = lax.broadcasted_iota(jnp.int32, (LANES, LANES), 0)
        cc = lax.broadcasted_iota(jnp.int32, (LANES, LANES), 1)
        incl = jnp.where(rr <= cc, 1.0, 0.0).astype(BF16)
        bias = -LOG2E * lane_sums(lfn_ref[0], incl)
        bias = jnp.concatenate([bias] * n_new, axis=0)
        j_idx = lax.broadcasted_iota(jnp.int32, (n_rows, LANES), 1)
        t_idx = lax.broadcasted_iota(jnp.int32, (n_rows, LANES), 0) >> 4
        kn = kn_ref[0]
        vn = vn_ref[0]
        s = jnp.full((n_rows, LANES), MASK_VALUE, F32)
        for j in range(n_new):
            sj = jnp.sum(qb * kn[j:j + 1, :], axis=-1, keepdims=True)
            s = jnp.where(j_idx == j, sj, s)
        s = jnp.where(j_idx <= t_idx, s + bias, MASK_VALUE)
        m = jnp.max(s, axis=-1, keepdims=True)
        p = jnp.exp2(s - m)
        m_ref[...] = jnp.broadcast_to(m, (n_rows, LANES))
        l_ref[...] = jnp.broadcast_to(jnp.sum(p, axis=-1, keepdims=True), (n_rows, LANES))
        acc = jnp.zeros((n_rows, 1024), F32)
        for j in range(n_new):
            acc = acc + p[:, j:j + 1] * vn[j:j + 1, :]
        acc_ref[...] = acc
        r_ref[...] = jnp.zeros_like(r_ref)

    r = r_ref[...]
    biases = []
    for i in range(group):
        kbf_ref[:, i * LANES:(i + 1) * LANES] = k_refs[i][0, 0].astype(BF16)
        vbf_ref[:, i * LANES:(i + 1) * LANES] = v_refs[i][0, 0].astype(BF16)
        lf = lf_refs[i][0, 0]
        biases.append(lane_sums(lf, tri_ref[...]) + r)
        r = r + jnp.sum(lf, axis=-1, keepdims=True)
    r_ref[...] = r
    bias = LOG2E * jnp.concatenate(biases, axis=1)
    s = _dot(qbd_ref[...], kbf_ref[...]) + jnp.concatenate([bias] * n_new, axis=0)
    m_prev = m_ref[...]
    m_new = jnp.maximum(m_prev, jnp.max(s, axis=-1, keepdims=True))
    alpha = jnp.exp2(m_prev - m_new)
    p = jnp.exp2(s - widen(m_new, group))
    l_ref[...] = alpha * l_ref[...] + jnp.sum(p, axis=-1, keepdims=True)
    acc_ref[...] = widen(alpha, 1024 // LANES) * acc_ref[...] + _dot_nt(p.astype(BF16), vbf_ref[...])
    m_ref[...] = m_new

    @pl.when(step == pl.num_programs(1) - 1)
    def _():
        o = jnp.where(own_head, acc_ref[...] / widen(l_ref[...], 1024 // LANES), 0.0)
        for t in range(n_new):
            o_ref[0, t:t + 1, :] = jnp.sum(o[t * N_HEADS:(t + 1) * N_HEADS], axis=0, keepdims=True)


def _fox_decode(page_table, q, kn, vn, lfn, cache_kt, cache_vt, cache_lft, layer, group):
    n_b, n_pages = page_table.shape
    n_new = q.shape[1]
    rr = lax.broadcasted_iota(jnp.int32, (LANES, LANES), 0)
    cc = lax.broadcasted_iota(jnp.int32, (LANES, LANES), 1)
    strict = jnp.where(rr > cc, 1.0, 0.0).astype(BF16)

    def page_spec(shape, i):
        return pl.BlockSpec((1, 1) + shape,
                            lambda b, s, pt: (layer, pt[b, n_pages - 1 - (s * group + i)], 0, 0))

    per_b = lambda shape: pl.BlockSpec((1,) + shape, lambda b, s, pt: (b,) + (0,) * len(shape))
    n_rows = n_new * N_HEADS
    grid_spec = pltpu.PrefetchScalarGridSpec(
        num_scalar_prefetch=1,
        grid=(n_b, n_pages // group),
        in_specs=([per_b((n_new, 1024)), per_b((n_new, 1024)), per_b((n_new, 1024)), per_b((N_HEADS, LANES)),
                   pl.BlockSpec((LANES, LANES), lambda b, s, pt: (0, 0))]
                  + [page_spec((1024, LANES), i) for i in range(group)]
                  + [page_spec((1024, LANES), i) for i in range(group)]
                  + [page_spec((N_HEADS, LANES), i) for i in range(group)]),
        out_specs=per_b((n_new, 1024)),
        scratch_shapes=[pltpu.VMEM((n_rows, 1024), BF16),
                        pltpu.VMEM((1024, LANES * group), BF16),
                        pltpu.VMEM((1024, LANES * group), BF16),
                        pltpu.VMEM((n_rows, LANES), F32),
                        pltpu.VMEM((n_rows, LANES), F32),
                        pltpu.VMEM((n_rows, 1024), F32),
                        pltpu.VMEM((N_HEADS, LANES), F32)])
    return pl.pallas_call(
        functools.partial(_fox_decode_kernel, n_new=n_new, group=group),
        grid_spec=grid_spec,
        out_shape=jax.ShapeDtypeStruct((n_b, n_new, 1024), F32),
        compiler_params=_params("arbitrary", "arbitrary"),
        name="fox_decode",
    )(page_table, q, kn, vn, lfn, strict,
      *([cache_kt] * group), *([cache_vt] * group), *([cache_lft] * group))


def _fox_out_kernel(o_ref, g_ref, x_ref, w_ref, lng_ref, lnb_ref, y_ref):
    a = (o_ref[...] * _silu(g_ref[...])).astype(BF16)
    z = DN_ALPHA * x_ref[...] + _dot(a, w_ref[...])
    y_ref[...] = _layer_norm(z, lng_ref[...], lnb_ref[...])


def _fox_out(o2d, g2d, x2d, w_out, ln_g, ln_b, tm):
    m = x2d.shape[0]
    wide = pl.BlockSpec((tm, 1024), lambda i: (i, 0))
    return pl.pallas_call(
        _fox_out_kernel,
        grid=(m // tm,),
        in_specs=[wide, wide, wide, _const_spec((1024, 1024)), _const_spec((1, 1024)),
                  _const_spec((1, 1024))],
        out_specs=wide,
        out_shape=jax.ShapeDtypeStruct((m, 1024), F32),
        compiler_params=_params("arbitrary"),
        name="fox_out",
    )(o2d, g2d, x2d, w_out, ln_g, ln_b)


def _pool_tail(m_bf, gate, x, wg_ref, bg_ref, sc_ref, wo_ref, lng_ref, lnb_ref):
    z = jnp.concatenate([_dot(m_bf[:, g * GROUP_B:(g + 1) * GROUP_B], wg_ref[g])
                         for g in range(len(POOL_WINDOWS))], axis=1) + bg_ref[...]
    a = (z * sc_ref[...] * _silu(gate)).astype(BF16)
    zz = DN_ALPHA * x + _dot(a, wo_ref[...])
    return _layer_norm(zz, lng_ref[...], lnb_ref[...])


def _pool_prompt_kernel(x_ref, wi_ref, wg_ref, bg_ref, sc_ref, wo_ref, lng_ref, lnb_ref,
                        y_ref, hist_ref, ext_ref, m_ref, *, tm):
    i = pl.program_id(1)

    @pl.when(i == 0)
    def _():
        ext_ref[0:HIST_PAD, :] = jnp.zeros((HIST_PAD, WIDTH_B), F32)

    x = x_ref[...]
    xb = x.astype(BF16)
    a = _dot(xb, wi_ref[:, 0:WIDTH_B])
    ext_ref[HIST_PAD:, :] = a
    for g, w in enumerate(POOL_WINDOWS):
        cols = slice(g * GROUP_B, (g + 1) * GROUP_B)
        win = a[:, cols]
        for d in range(1, w):
            win = win + ext_ref[HIST_PAD - d:HIST_PAD - d + tm, cols]
        m_ref[:, cols] = (win * (1.0 / w) - a[:, cols]).astype(BF16)

    @pl.when(i == 0)
    def _():
        pos = lax.broadcasted_iota(jnp.int32, (HIST_PAD, GROUP_B), 0)
        for g, w in enumerate(POOL_WINDOWS):
            cols = slice(g * GROUP_B, (g + 1) * GROUP_B)
            win = ext_ref[HIST_PAD:2 * HIST_PAD, cols]
            for d in range(1, w):
                win = win + ext_ref[HIST_PAD - d:2 * HIST_PAD - d, cols]
            cnt = jnp.minimum(pos + 1, w).astype(F32)
            m_ref[0:HIST_PAD, cols] = (win / cnt - ext_ref[HIST_PAD:2 * HIST_PAD, cols]).astype(BF16)

    gate = _dot(xb, wi_ref[:, WIDTH_B:])
    y_ref[...] = _pool_tail(m_ref[...], gate, x, wg_ref, bg_ref, sc_ref, wo_ref, lng_ref, lnb_ref)
    hist_ref[0] = a[tm - HIST_PAD:, :]
    ext_ref[0:HIST_PAD, :] = a[tm - HIST_PAD:, :]


def _pool_prompt(x2d, w_in, w_grp, b_grp, scale, w_out, ln_g, ln_b, n_seq, tm):
    m = x2d.shape[0]
    nt = m // (n_seq * tm)
    wide = pl.BlockSpec((tm, 1024), lambda b, i: (b * nt + i, 0))
    return pl.pallas_call(
        functools.partial(_pool_prompt_kernel, tm=tm),
        grid=(n_seq, nt),
        in_specs=[wide, _const_spec((1024, 2 * WIDTH_B)), _const_spec((4, GROUP_B, GROUP_B)),
                  _const_spec((1, WIDTH_B)), _const_spec((1, WIDTH_B)), _const_spec((WIDTH_B, 1024)),
                  _const_spec((1, 1024)), _const_spec((1, 1024))],
        out_specs=[wide, pl.BlockSpec((1, HIST_PAD, WIDTH_B), lambda b, i: (b, 0, 0))],
        out_shape=[jax.ShapeDtypeStruct((m, 1024), F32),
                   jax.ShapeDtypeStruct((n_seq, HIST_PAD, WIDTH_B), F32)],
        scratch_shapes=[pltpu.VMEM((HIST_PAD + tm, WIDTH_B), F32),
                        pltpu.VMEM((tm, WIDTH_B), BF16)],
        compiler_params=_params("arbitrary", "arbitrary"),
        name="pool_prompt",
    )(x2d, w_in, w_grp, b_grp, scale, w_out, ln_g, ln_b)


def _pool_sample_kernel(x_ref, hist_ref, wi_ref, wg_ref, bg_ref, sc_ref, wo_ref, lng_ref, lnb_ref,
                        y_ref, newhist_ref, *, n_new):
    xs = [x_ref[t] for t in range(n_new)]
    a = [_dot(x.astype(BF16), wi_ref[:, 0:WIDTH_B]) for x in xs]
    ext = [hist_ref[r] for r in range(POOL_HIST)] + a
    for r in range(POOL_HIST):
        newhist_ref[r] = ext[n_new + r]
    for t in range(n_new):
        parts = []
        for g, w in enumerate(POOL_WINDOWS):
            cols = slice(g * GROUP_B, (g + 1) * GROUP_B)
            win = a[t][:, cols]
            for d in range(1, w):
                win = win + ext[POOL_HIST + t - d][:, cols]
            parts.append(win * (1.0 / w) - a[t][:, cols])
        m_bf = jnp.concatenate(parts, axis=1).astype(BF16)
        gate = _dot(xs[t].astype(BF16), wi_ref[:, WIDTH_B:])
        y_ref[t] = _pool_tail(m_bf, gate, xs[t], wg_ref, bg_ref, sc_ref, wo_ref, lng_ref, lnb_ref)


def _pool_sample(x_tm, hist_tm, w_in, w_grp, b_grp, scale, w_out, ln_g, ln_b):
    n_new, n_b, _ = x_tm.shape
    full = lambda a: _const_spec(a.shape)
    args = (x_tm, hist_tm, w_in, w_grp, b_grp, scale, w_out, ln_g, ln_b)
    return pl.pallas_call(
        functools.partial(_pool_sample_kernel, n_new=n_new),
        grid=(1,),
        in_specs=[full(a) for a in args],
        out_specs=[_const_spec((n_new, n_b, 1024)), _const_spec((POOL_HIST, n_b, WIDTH_B))],
        out_shape=[jax.ShapeDtypeStruct((n_new, n_b, 1024), F32),
                   jax.ShapeDtypeStruct((POOL_HIST, n_b, WIDTH_B), F32)],
        compiler_params=_params("arbitrary"),
        name="pool_sample",
    )(*args)


def _gmlp_kernel(x_ref, wi_ref, vg_ref, vb_ref, ws_ref, bs_ref, wo_ref, lng_ref, lnb_ref,
                 *out_refs, tm, span, emit_v):
    y_ref = out_refs[0]
    x = x_ref[...]
    xb = x.astype(BF16)
    v = _layer_norm(_dot(xb, wi_ref[:, WIDTH_C:2 * WIDTH_C]), vg_ref[...], vb_ref[...])
    if emit_v:
        out_refs[1][...] = v
    vb16 = v.astype(BF16)
    r = lax.broadcasted_iota(jnp.int32, (CHUNK, CHUNK), 0)
    c = lax.broadcasted_iota(jnp.int32, (CHUNK, CHUNK), 1)
    keep = (c <= r) & ((r ^ c) < span)
    chunks = []
    for n in range(tm // CHUNK):
        rows = slice(n * CHUNK, (n + 1) * CHUNK)
        groups = []
        for g in range(N_GROUPS_C):
            wmix = jnp.where(keep, ws_ref[g], 0.0).astype(BF16)
            groups.append(_dot(wmix, vb16[rows, g * GROUP_C:(g + 1) * GROUP_C]))
        chunks.append(jnp.concatenate(groups, axis=1) + bs_ref[...])
    sv = jnp.concatenate(chunks, axis=0) if len(chunks) > 1 else chunks[0]
    u = _dot(xb, wi_ref[:, 0:WIDTH_C])
    gate = _dot(xb, wi_ref[:, 2 * WIDTH_C:])
    a = (u * sv * _silu(gate)).astype(BF16)
    z = DN_ALPHA * x + _dot(a, wo_ref[...])
    y_ref[...] = _layer_norm(z, lng_ref[...], lnb_ref[...])


def _gmlp(x2d, w_in, vg, vb, ws_tiled, bs_rows, w_out, ln_g, ln_b, tm, span, emit_v):
    m = x2d.shape[0]
    wide = pl.BlockSpec((tm, 1024), lambda i: (i, 0))
    out_specs = [wide]
    out_shape = [jax.ShapeDtypeStruct((m, 1024), F32)]
    if emit_v:
        out_specs.append(pl.BlockSpec((tm, WIDTH_C), lambda i: (i, 0)))
        out_shape.append(jax.ShapeDtypeStruct((m, WIDTH_C), F32))
    return pl.pallas_call(
        functools.partial(_gmlp_kernel, tm=tm, span=span, emit_v=emit_v),
        grid=(m // tm,),
        in_specs=[wide, _const_spec((1024, 3 * WIDTH_C)), _const_spec((1, WIDTH_C)),
                  _const_spec((1, WIDTH_C)), _const_spec((N_GROUPS_C, CHUNK, CHUNK)),
                  _const_spec((CHUNK, WIDTH_C)), _const_spec((WIDTH_C, 1024)),
                  _const_spec((1, 1024)), _const_spec((1, 1024))],
        out_specs=out_specs,
        out_shape=out_shape,
        compiler_params=_params("arbitrary"),
        name="gmlp_v" if emit_v else "gmlp",
    )(x2d, w_in, vg, vb, ws_tiled, bs_rows, w_out, ln_g, ln_b)


def _row(v):
    return v.reshape(1, -1)


def _fox_layer(xp, xs, prev_kt, prev_vt, cache_kt, cache_vt, cache_lft, page_table, layer,
               w_in, b_f, w_out, ln_g, ln_b):
    n_seq, seq, _ = xp.shape
    n_b, n_new, _ = xs.shape
    w_main = w_in[:, :4 * 1024].astype(BF16)
    w_f = jnp.pad(jnp.tile(w_in[:, 4 * 1024:], (1, 3)), ((0, 0), (0, LANES - 3 * N_HEADS))).astype(BF16)
    b_fp = jnp.pad(jnp.tile(b_f, 3), (0, LANES - 3 * N_HEADS)).reshape(1, LANES)
    w_o = w_out.astype(BF16)

    x2d = xp.reshape(n_seq * seq, 1024)
    qt, kt32, vt32, kbf, gate, logf, caug = _fox_proj_prompt(
        x2d, w_main[:, :3 * 1024].T, w_main[:, 1024:2048], w_main[:, 3 * 1024:], w_f, b_fp,
        prev_kt, prev_vt, n_seq, 512)
    o = _fox_attn(qt, kbf.reshape(n_seq, seq, 1024), caug.reshape(n_seq, seq, LANES), vt32, 512, 4, 1)
    yp = _fox_out(o.reshape(n_seq * seq, 1024), gate, x2d, w_o, ln_g, ln_b, 512)
    out_p = (kt32, vt32, logf.reshape(n_seq, seq, N_HEADS))

    rows = n_b * n_new
    xs2d = xs.reshape(rows, 1024)
    qs, ks32, vs32, gs, lfs = _fox_proj_rows(xs2d, w_main, w_f, b_fp)
    lfn = jnp.pad(lfs.reshape(n_b, n_new, N_HEADS).transpose(0, 2, 1),
                  ((0, 0), (0, 0), (0, LANES - n_new)))
    od = _fox_decode(page_table, qs.reshape(n_b, n_new, 1024), ks32.reshape(n_b, n_new, 1024),
                     vs32.reshape(n_b, n_new, 1024), lfn, cache_kt, cache_vt, cache_lft, layer, 16)
    ys = _fox_out(od.reshape(rows, 1024), gs, xs2d, w_o, ln_g, ln_b, rows)
    out_s = (ks32.reshape(n_b, n_new, N_HEADS, HEAD_DIM), vs32.reshape(n_b, n_new, N_HEADS, HEAD_DIM),
             lfs.reshape(n_b, n_new, N_HEADS))
    return yp.reshape(n_seq, seq, 1024), ys.reshape(n_b, n_new, 1024), out_p, out_s


def _pool_layer(xp, xs, hist, w_in, w_grp, b_grp, scale, w_out, ln_g, ln_b):
    n_seq, seq, _ = xp.shape
    n_b, n_new, _ = xs.shape
    weights = (w_in.astype(BF16), w_grp.astype(BF16), _row(b_grp), _row(scale), w_out.astype(BF16),
               ln_g, ln_b)
    yp, hp = _pool_prompt(xp.reshape(n_seq * seq, 1024), *weights, n_seq, 256)
    ys, hs = _pool_sample(xs.transpose(1, 0, 2), hist.transpose(1, 0, 2), *weights)
    return (yp.reshape(n_seq, seq, 1024), ys.transpose(1, 0, 2),
            hp[:, HIST_PAD - POOL_HIST:, :], hs.transpose(1, 0, 2))


def _gmlp_layer(xp, xs, w_in, vg, vb, w_s, b_s, w_out, ln_g, ln_b):
    n_seq, seq, _ = xp.shape
    n_b, n_new, _ = xs.shape
    rows = n_b * n_new
    shared = (w_in.astype(BF16), _row(vg), _row(vb))
    tail = (w_out.astype(BF16), ln_g, ln_b)
    bs_p = jnp.repeat(b_s.T, GROUP_C, axis=1)
    yp, = _gmlp(xp.reshape(n_seq * seq, 1024), *shared, w_s, bs_p, *tail, 256, CHUNK, False)
    reps = CHUNK // n_new
    ws_s = jnp.tile(w_s[:, :n_new, :n_new], (1, reps, reps))
    bs_s = jnp.tile(jnp.repeat(b_s[:, :n_new].T, GROUP_C, axis=1), (reps, 1))
    ys, vrows = _gmlp(xs.reshape(rows, 1024), *shared, ws_s, bs_s, *tail, rows, n_new, True)
    return yp.reshape(n_seq, seq, 1024), ys.reshape(n_b, n_new, 1024), vrows.reshape(n_b, n_new, WIDTH_C)


def _rows_major(t):
    return t.reshape(t.shape[0], t.shape[1], N_HEADS, HEAD_DIM, t.shape[3]).transpose(0, 1, 4, 2, 3)


def kernel(x_prompt, x_sample, cache_k, cache_v, cache_logf, state_pool, page_table, ln_g, ln_b, w_in_a, b_f_a, w_out_a, w_in_b, w_grp_b, b_grp_b, scale_b, w_out_b, w_in_c, ln_v_g, ln_v_b, w_s_c, b_s_c, w_out_c):
    n_layers_a, n_pool, page_size = cache_k.shape[:3]
    ck = cache_k.transpose(0, 1, 3, 4, 2).reshape(n_layers_a, n_pool, N_HEADS * HEAD_DIM, page_size)
    cv = cache_v.transpose(0, 1, 3, 4, 2).reshape(n_layers_a, n_pool, N_HEADS * HEAD_DIM, page_size)
    clft = jnp.swapaxes(cache_logf, 2, 3)
    xp, xs = x_prompt, x_sample
    kt_stack, vt_stack = None, None
    fp_l, ks_l, vs_l, fs_l = [], [], [], []
    poolp_l, pools_l, chunkv_l = [], [], []
    for i in range(DEPTH):
        j = i // N_MIXERS
        kind = i % N_MIXERS
        g, b = _row(ln_g[i]), _row(ln_b[i])
        if kind == 0:
            xp, xs, out_p, out_s = _fox_layer(xp, xs, kt_stack, vt_stack, ck, cv, clft, page_table, j,
                                              w_in_a[j], b_f_a[j], w_out_a[j], g, b)
            kt_stack, vt_stack = out_p[0], out_p[1]
            fp_l.append(out_p[2])
            ks_l.append(out_s[0]); vs_l.append(out_s[1]); fs_l.append(out_s[2])
        elif kind == 1:
            xp, xs, hp, hs = _pool_layer(xp, xs, state_pool[j], w_in_b[j], w_grp_b[j], b_grp_b[j],
                                         scale_b[j], w_out_b[j], g, b)
            poolp_l.append(hp); pools_l.append(hs)
        else:
            xp, xs, vrows = _gmlp_layer(xp, xs, w_in_c[j], ln_v_g[j], ln_v_b[j], w_s_c[j], b_s_c[j],
                                        w_out_c[j], g, b)
            chunkv_l.append(vrows)
    return (xp, xs,
            _rows_major(kt_stack), _rows_major(vt_stack), jnp.stack(fp_l),
            jnp.stack(ks_l), jnp.stack(vs_l), jnp.stack(fs_l),
            jnp.stack(poolp_l), jnp.stack(pools_l),
            jnp.stack(chunkv_l))
```
